```python
import jax, jax.numpy as jnp
from jax import lax
import numpy as np

D_MODEL = 1024
BATCH = 1
SEQ = 16384
DEPTH = 1
DEC_BATCH = 32
DEC_SEQ = 2048
PAST_LEN = 128

HEAD_DIM = 64
ATTN_WIDTH = D_MODEL // 2
ATTN_HEADS = ATTN_WIDTH // HEAD_DIM
RWKV_WIDTH = D_MODEL - ATTN_WIDTH
RWKV_HEADS = RWKV_WIDTH // HEAD_DIM
DECAY_RANK = 64
ICLR_RANK = 64
GATE_RANK = 128
D_FF = 4 * D_MODEL
DILATION_PATTERNS = ((128, 1), (512, 4), (2048, 16))
ATTN_IN = 3 * ATTN_WIDTH
RWKV_IN = 3 * RWKV_WIDTH + 2 * DECAY_RANK + 2 * ICLR_RANK + GATE_RANK
IN_WIDTH = ATTN_IN + RWKV_IN
NORM_EPS = 1e-6
LN_X_EPS = 64e-5
NEG_INF = -1e30

kernel_name = 'hymba_longnet_rwkv7_adaln_encoder'


def rmsnorm(x, g):
    x32 = x.astype(jnp.float32)
    y = x32 * lax.rsqrt(jnp.mean(x32 * x32, axis=-1, keepdims=True) + NORM_EPS) * g
    return y.astype(x.dtype)


def alibi_slopes(n_heads):
    return 2.0 ** (-8.0 * (jnp.arange(n_heads, dtype=jnp.float32) + 1.0) / n_heads)


def dilated_band_attention(q, k, v, dil, half, slopes):
    B, S, H, Dh = q.shape
    L = S // dil
    nb = -(-L // half)
    Lp = nb * half
    N = B * dil

    def to_sub(z):
        return z.reshape(B, L, dil, H, Dh).transpose(0, 2, 1, 3, 4).reshape(N, L, H, Dh)

    def band(z):
        zp = jnp.pad(z, ((0, 0), (half, Lp - L + half), (0, 0), (0, 0))).reshape(N, nb + 2, half, H, Dh)
        return jnp.concatenate([zp[:, :-2], zp[:, 1:-1], zp[:, 2:]], axis=2)

    qb = jnp.pad(to_sub(q), ((0, 0), (0, Lp - L), (0, 0), (0, 0))).reshape(N, nb, half, H, Dh)
    kb = band(to_sub(k))
    vb = band(to_sub(v))

    qi = jnp.arange(half)[:, None]
    kj = jnp.arange(3 * half)[None, :]
    rel = kj - half - qi
    kpos = jnp.arange(nb)[:, None, None] * half + kj[None] - half
    valid = (jnp.abs(rel)[None] <= half) & (kpos >= 0) & (kpos < L)
    bias = -(slopes[:, None, None] * (dil * jnp.abs(rel)).astype(jnp.float32))

    s = jnp.einsum('nbqhd,nbkhd->nbhqk', qb, kb).astype(jnp.float32) + bias[None, None]
    s = jnp.where(valid[None, :, None], s, NEG_INF)
    m = jnp.max(s, axis=-1, keepdims=True)
    p = jnp.exp(s - m)
    l = jnp.sum(p, axis=-1)
    o = jnp.einsum('nbhqk,nbkhd->nbqhd', p.astype(vb.dtype), vb).astype(jnp.float32)
    o = o / jnp.swapaxes(l, 2, 3)[..., None]
    lse = jnp.swapaxes(m[..., 0] + jnp.log(l), 2, 3)

    def from_sub(z):
        tail = z.shape[3:]
        z = z.reshape(B, dil, Lp, *tail)[:, :, :L]
        return jnp.swapaxes(z, 1, 2).reshape(B, S, *tail)

    return from_sub(o), from_sub(lse)


def head_rmsnorm(z, g):
    z32 = z.astype(jnp.float32)
    return (z32 * lax.rsqrt(jnp.mean(z32 * z32, axis=-1, keepdims=True) + NORM_EPS) * g).astype(z.dtype)


def attention_mixer(p, q_norm_g, k_norm_g, attn_beta):
    B, S, _ = p.shape
    q, k, v = (z.reshape(B, S, ATTN_HEADS, HEAD_DIM) for z in jnp.split(p, 3, axis=-1))
    q = head_rmsnorm(q, q_norm_g) * (HEAD_DIM ** -0.5)
    k = head_rmsnorm(k, k_norm_g)
    slopes = alibi_slopes(ATTN_HEADS)
    outs, lses = [], []
    for window, dil in DILATION_PATTERNS:
        o, lse = dilated_band_attention(q, k, v, dil, window // (2 * dil), slopes)
        outs.append(o)
        lses.append(lse)
    wts = jax.nn.softmax(jnp.stack(lses), axis=0)
    o = jnp.sum(wts[..., None] * jnp.stack(outs), axis=0)
    return (o.reshape(B, S, ATTN_WIDTH) * attn_beta).astype(p.dtype)


def _rwkv_step(state, inp):
    w, kk, b, k, v, r = inp
    sa = jnp.einsum('dbhvk,dbhk->dbhv', state, kk)
    state = state * w[..., None, :] - sa[..., :, None] * b[..., None, :] + v[..., :, None] * k[..., None, :]
    y = jnp.einsum('dbhvk,dbhk->dbhv', state, r)
    return state, y


def rwkv7_mixer(p, mu_prev, mu_next, w0, w_up, a0, a_up, g_up, k_k, k_a, r_k, ln_x_w, ln_x_b):
    B, S, _ = p.shape
    f32 = jnp.float32
    prev = jnp.pad(p[:, :-1], ((0, 0), (1, 0), (0, 0)))
    nxt = jnp.pad(p[:, 1:], ((0, 0), (0, 1), (0, 0)))
    p = p + mu_prev * (prev - p) + mu_next * (nxt - p)
    cuts = tuple(int(c) for c in np.cumsum([RWKV_WIDTH] * 3 + [DECAY_RANK] * 2 + [ICLR_RANK] * 2))
    r, k, v, wd_f, wd_b, ad_f, ad_b, gd = jnp.split(p, cuts, axis=-1)
    wd = jnp.stack([wd_f, wd_b])
    ad = jnp.stack([ad_f, ad_b])
    w_raw = (w0[:, None, None] + jnp.einsum('dbsr,drc->dbsc', jnp.tanh(wd), w_up)).astype(f32)
    decay = jnp.exp(-jnp.exp(-jax.nn.softplus(-w_raw) - 0.5))
    a = jax.nn.sigmoid((a0[:, None, None] + jnp.einsum('dbsr,drc->dbsc', ad, a_up)).astype(f32))
    g = jax.nn.sigmoid(gd) @ g_up

    def heads(z):
        return z.reshape(*z.shape[:-1], RWKV_HEADS, HEAD_DIM)

    r32, k32, v32 = r.astype(f32), k.astype(f32), v.astype(f32)
    kk = heads(k32 * k_k)
    kk = kk * lax.rsqrt(jnp.maximum(jnp.sum(kk * kk, axis=-1, keepdims=True), 1e-24))
    kd = heads(k32[None] * (1.0 + (a - 1.0) * k_a))
    bd = kk[None] * heads(a)
    rh, vh = heads(r32), heads(v32)

    def both(z):
        return jnp.stack([z, jnp.flip(z, 1)])

    def dirs(z):
        return jnp.stack([z[0], jnp.flip(z[1], 1)])

    seqs = (dirs(heads(decay)), both(kk), dirs(bd), dirs(kd), both(vh), both(rh))
    seqs = tuple(jnp.moveaxis(z, 2, 0) for z in seqs)
    state0 = jnp.zeros((2, B, RWKV_HEADS, HEAD_DIM, HEAD_DIM), f32)
    _, ys = lax.scan(_rwkv_step, state0, seqs)
    ys = jnp.moveaxis(ys, 0, 2)
    y = ys[0] + jnp.flip(ys[1], 1)
    mu = jnp.mean(y, axis=-1, keepdims=True)
    var = jnp.mean(jnp.square(y - mu), axis=-1, keepdims=True)
    yn = ((y - mu) * lax.rsqrt(var + LN_X_EPS)).reshape(B, S, RWKV_WIDTH) * ln_x_w + ln_x_b
    bonus = jnp.sum(rh * (kd[0] + kd[1]) * r_k, axis=-1, keepdims=True) * vh
    out = (yn + bonus.reshape(B, S, RWKV_WIDTH)) * g
    return out.astype(p.dtype)


def encoder_layer(x, c, w_ada, b_ada, g_norm1, g_norm2, w_in, q_norm_g, k_norm_g, attn_beta,
                  mu_prev, mu_next, w0, w_up, a0, a_up, g_up, k_k, k_a, r_k, ln_x_w, ln_x_b,
                  w_out, w_ff1, w_ff2):
    mod = jax.nn.silu(c) @ w_ada + b_ada
    sh1, sc1, gt1, sh2, sc2, gt2 = (m[:, None, :] for m in jnp.split(mod, 6, axis=-1))
    h = rmsnorm(x, g_norm1) * (1.0 + sc1) + sh1
    p = h @ w_in
    attn = attention_mixer(p[..., :ATTN_IN], q_norm_g, k_norm_g, attn_beta)
    rw = rwkv7_mixer(p[..., ATTN_IN:], mu_prev, mu_next, w0, w_up, a0, a_up, g_up,
                     k_k, k_a, r_k, ln_x_w, ln_x_b)
    x = x + gt1 * (jnp.concatenate([attn, rw], axis=-1) @ w_out)
    h = rmsnorm(x, g_norm2) * (1.0 + sc2) + sh2
    x = x + gt2 * (jnp.square(jax.nn.relu(h @ w_ff1)) @ w_ff2)
    return x


def setup_inputs(seed: int = 0) -> dict:
    key = jax.random.key(seed)
    ks = jax.random.split(key, 32)
    f32 = jnp.float32
    L = DEPTH

    def nrm(k, shape, scale):
        return jax.random.normal(k, shape, f32) * scale

    return {
        'x_prompt': nrm(ks[0], (BATCH, SEQ, D_MODEL), 1.0),
        'x_sample': nrm(ks[1], (DEC_BATCH, DEC_SEQ, D_MODEL), 1.0),
        'c_prompt': nrm(ks[2], (BATCH, D_MODEL), 1.0),
        'c_sample': nrm(ks[3], (DEC_BATCH, D_MODEL), 1.0),
        'w_ada': nrm(ks[4], (L, D_MODEL, 6 * D_MODEL), 0.5 * D_MODEL ** -0.5),
        'b_ada': nrm(ks[5], (L, 6 * D_MODEL), 0.02),
        'g_norm1': 1.0 + nrm(ks[6], (L, D_MODEL), 0.02),
        'g_norm2': 1.0 + nrm(ks[7], (L, D_MODEL), 0.02),
        'w_in': nrm(ks[8], (L, D_MODEL, IN_WIDTH), D_MODEL ** -0.5),
        'q_norm_g': 1.0 + nrm(ks[9], (L, HEAD_DIM), 0.02),
        'k_norm_g': 1.0 + nrm(ks[10], (L, HEAD_DIM), 0.02),
        'attn_beta': 1.0 + nrm(ks[11], (L, ATTN_WIDTH), 0.02),
        'mu_prev': jax.random.uniform(ks[12], (L, RWKV_IN), f32, 0.0, 0.5),
        'mu_next': jax.random.uniform(ks[13], (L, RWKV_IN), f32, 0.0, 0.5),
        'w0': jax.random.uniform(ks[14], (L, 2, RWKV_WIDTH), f32, -5.0, 0.0),
        'w_up': nrm(ks[15], (L, 2, DECAY_RANK, RWKV_WIDTH), 0.1),
        'a0': nrm(ks[16], (L, 2, RWKV_WIDTH), 0.5),
        'a_up': nrm(ks[17], (L, 2, ICLR_RANK, RWKV_WIDTH), 0.1),
        'g_up': nrm(ks[18], (L, GATE_RANK, RWKV_WIDTH), GATE_RANK ** -0.5),
        'k_k': 0.85 + nrm(ks[19], (L, RWKV_WIDTH), 0.05),
        'k_a': 1.0 + nrm(ks[20], (L, RWKV_WIDTH), 0.05),
        'r_k': nrm(ks[21], (L, RWKV_HEADS, HEAD_DIM), 0.1),
        'ln_x_w': 1.0 + nrm(ks[22], (L, RWKV_WIDTH), 0.02),
        'ln_x_b': nrm(ks[23], (L, RWKV_WIDTH), 0.02),
        'w_out': nrm(ks[24], (L, D_MODEL, D_MODEL), D_MODEL ** -0.5),
        'w_ff1': nrm(ks[25], (L, D_MODEL, D_FF), D_MODEL ** -0.5),
        'w_ff2': nrm(ks[26], (L, D_FF, D_MODEL), 0.5 * D_FF ** -0.5),
    }


def reference(x_prompt, x_sample, c_prompt, c_sample, w_ada, b_ada, g_norm1, g_norm2, w_in,
              q_norm_g, k_norm_g, attn_beta, mu_prev, mu_next, w0, w_up, a0, a_up, g_up,
              k_k, k_a, r_k, ln_x_w, ln_x_b, w_out, w_ff1, w_ff2):
    y_prompt = x_prompt
    y_sample = x_sample
    for i in range(DEPTH):
        layer_params = (w_ada[i], b_ada[i], g_norm1[i], g_norm2[i], w_in[i], q_norm_g[i], k_norm_g[i],
                        attn_beta[i], mu_prev[i], mu_next[i], w0[i], w_up[i], a0[i], a_up[i], g_up[i],
                        k_k[i], k_a[i], r_k[i], ln_x_w[i], ln_x_b[i], w_out[i], w_ff1[i], w_ff2[i])
        y_prompt = encoder_layer(y_prompt, c_prompt, *layer_params)
        y_sample = encoder_layer(y_sample, c_sample, *layer_params)
    return (y_prompt, y_sample)
```

```python
import functools

import numpy as np
import jax
import jax.numpy as jnp
from jax import lax
from jax.experimental import pallas as pl
from jax.experimental.pallas import tpu as pltpu

F32 = jnp.float32
BF16 = jnp.bfloat16

D_MODEL = 1024
HEAD_DIM = 64
ATTN_WIDTH = 512
RWKV_WIDTH = 512
N_PAIRS = ATTN_WIDTH // 128
RWKV_IN = 1920
IN_WIDTH = 3 * ATTN_WIDTH + RWKV_IN
D_FF = 4096
DILATIONS = (1, 4, 16)
HALF = 64
NORM_EPS = 1e-6
LN_X_EPS = 64e-5
NEG_INF = -1e30

SEQ_BLOCK = 2048
HALO = 1024
SCAN_BLOCK = 512
CHUNK = 64
ROW_TILE = 512
PREP_TILE = 256

VMEM_LIMIT = 56 * 1024 * 1024


def _dot(a, b):
    return jnp.dot(a, b, preferred_element_type=F32)


def _dot_nt(a, b):
    return lax.dot_general(a, b, (((1,), (1,)), ((), ())), preferred_element_type=F32)


def _dot_tn(a, b):
    return lax.dot_general(a, b, (((0,), (0,)), ((), ())), preferred_element_type=F32)


def _split3(x):
    hi = x.astype(BF16)
    r1 = x - hi.astype(F32)
    mid = r1.astype(BF16)
    lo = (r1 - mid.astype(F32)).astype(BF16)
    return hi, mid, lo


def _sigmoid(x):
    return 1.0 / (1.0 + jnp.exp(-x))


def _ada_kernel(c_ref, w_ref, b_ref, o_ref):
    c = c_ref[...]
    s = c * _sigmoid(c)
    s1, s2, _ = _split3(s)
    w = w_ref[...]
    w1, w2, _ = _split3(w)
    o_ref[...] = _dot(s1, w1) + (_dot(s1, w2) + _dot(s2, w1)) + b_ref[...]


def _ada(c_all, w_ada, b_ada):
    n = c_all.shape[0]
    nt = w_ada.shape[1] // D_MODEL
    return pl.pallas_call(
        _ada_kernel,
        grid=(nt,),
        in_specs=[pl.BlockSpec((n, D_MODEL), lambda j: (0, 0)),
                  pl.BlockSpec((D_MODEL, D_MODEL), lambda j: (0, j)),
                  pl.BlockSpec((1, D_MODEL), lambda j: (0, j))],
        out_specs=pl.BlockSpec((n, D_MODEL), lambda j: (0, j)),
        out_shape=jax.ShapeDtypeStruct((n, w_ada.shape[1]), F32),
        compiler_params=pltpu.CompilerParams(dimension_semantics=("arbitrary",), vmem_limit_bytes=VMEM_LIMIT),
        name="ada",
    )(c_all, w_ada, b_ada.reshape(1, -1))


def _modulated_rmsnorm(x, g, scale, shift):
    ms = jnp.mean(x * x, axis=-1, keepdims=True)
    return (x * lax.rsqrt(ms + NORM_EPS) * g) * (1.0 + scale) + shift


def _inproj_kernel(mrow_ref, x_ref, g_ref, sh_ref, sc_ref, w_ref, q_ref, k_ref, v_ref, p_ref):
    del mrow_ref
    h = _modulated_rmsnorm(x_ref[...], g_ref[...], sc_ref[0], sh_ref[0]).astype(BF16)
    a = ATTN_WIDTH
    q_ref[...] = _dot(h, w_ref[:, 0:a])
    k_ref[...] = _dot(h, w_ref[:, a:2 * a])
    v_ref[...] = _dot(h, w_ref[:, 2 * a:3 * a])
    p_ref[...] = _dot(h, w_ref[:, 3 * a:])


def _inproj(x, mod3, mrow_tile, g_norm1, w_in_bf):
    ntok = x.shape[0]
    nt = ntok // ROW_TILE
    row = lambda i, mr: (i, 0)
    const = lambda i, mr: (0, 0)
    grid_spec = pltpu.PrefetchScalarGridSpec(
        num_scalar_prefetch=1,
        grid=(nt,),
        in_specs=[pl.BlockSpec((ROW_TILE, D_MODEL), row),
                  pl.BlockSpec((1, D_MODEL), const),
                  pl.BlockSpec((1, 1, D_MODEL), lambda i, mr: (mr[i] * 6 + 0, 0, 0)),
                  pl.BlockSpec((1, 1, D_MODEL), lambda i, mr: (mr[i] * 6 + 1, 0, 0)),
                  pl.BlockSpec((D_MODEL, IN_WIDTH), const)],
        out_specs=[pl.BlockSpec((ROW_TILE, ATTN_WIDTH), row)] * 3 + [pl.BlockSpec((ROW_TILE, RWKV_IN), row)],
    )
    return pl.pallas_call(
        _inproj_kernel,
        grid_spec=grid_spec,
        out_shape=[jax.ShapeDtypeStruct((ntok, ATTN_WIDTH), F32)] * 3 + [jax.ShapeDtypeStruct((ntok, RWKV_IN), F32)],
        compiler_params=pltpu.CompilerParams(dimension_semantics=("arbitrary",), vmem_limit_bytes=VMEM_LIMIT),
        name="inproj",
    )(mrow_tile, x, g_norm1.reshape(1, -1), mod3, mod3, w_in_bf)


def _lane_is_head0(shape):
    return lax.broadcasted_iota(jnp.int32, shape, len(shape) - 1) < HEAD_DIM


def _head_rmsnorm(x, gain):
    m0 = _lane_is_head0(x.shape)
    x2 = x * x
    s0 = jnp.sum(jnp.where(m0, x2, 0.0), axis=-1, keepdims=True)
    s1 = jnp.sum(jnp.where(m0, 0.0, x2), axis=-1, keepdims=True)
    ms = jnp.where(m0, s0, s1) * (1.0 / HEAD_DIM)
    return x * lax.rsqrt(ms + NORM_EPS) * gain


def _attn_kernel(hasprev_ref, hasnext_ref,
                 q_ref, kp_ref, kc_ref, kx_ref, vp_ref, vc_ref, vx_ref, qg_ref, kg_ref, beta_ref, bias_ref,
                 o_ref,
                 qn, kn, vn, qd, kd, vd, acc_o, acc_m, acc_l):
    i = pl.program_id(0)
    has_prev = hasprev_ref[i]
    has_next = hasnext_ref[i]
    rows = 256

    def fill(src_ref, dst_ref, dst_off, nrows, fn):
        def body(t, carry):
            r0 = pl.multiple_of(t * rows, rows)
            dst_ref[pl.ds(dst_off + r0, rows), :] = fn(src_ref[pl.ds(r0, rows), :])
            return carry
        lax.fori_loop(0, nrows // rows, body, 0)

    qg = qg_ref[...]
    kg = kg_ref[...]
    fill(q_ref, qn, 0, SEQ_BLOCK, lambda x: _head_rmsnorm(x, qg) * (HEAD_DIM ** -0.5))
    fill(kp_ref, kn, 0, HALO, lambda x: _head_rmsnorm(x, kg))
    fill(kc_ref, kn, HALO, SEQ_BLOCK, lambda x: _head_rmsnorm(x, kg))
    fill(kx_ref, kn, HALO + SEQ_BLOCK, HALO, lambda x: _head_rmsnorm(x, kg))
    fill(vp_ref, vn, 0, HALO, lambda x: x)
    fill(vc_ref, vn, HALO, SEQ_BLOCK, lambda x: x)
    fill(vx_ref, vn, HALO + SEQ_BLOCK, HALO, lambda x: x)

    m0_o = _lane_is_head0((HALF, 128))
    m0_q = _lane_is_head0((HALF, 128))
    col = lax.broadcasted_iota(jnp.int32, (1, 3 * HALF), 1)

    for di, dil in enumerate(DILATIONS):
        lq = SEQ_BLOCK // dil
        lk = lq + 2 * HALF
        nbq = lq // HALF
        kbase = HALO - HALF * dil

        if dil == 1:
            def cast_rows(src_ref, src_off, dst_ref, nrows, chunk):
                def body(t, carry):
                    r0 = pl.multiple_of(t * chunk, chunk)
                    dst_ref[pl.ds(r0, chunk), :] = src_ref[pl.ds(src_off + r0, chunk), :].astype(BF16)
                    return carry
                lax.fori_loop(0, nrows // chunk, body, 0)
            cast_rows(qn, 0, qd, lq, 256)
            cast_rows(kn, kbase, kd, lk, 128)
            cast_rows(vn, kbase, vd, lk, 128)
        else:
            def regroup(r, carry, dil=dil, lq=lq, lk=lk, kbase=kbase):
                qd[pl.ds(pl.multiple_of(r * lq, HALF), lq), :] = qn[pl.ds(r, lq, stride=dil), :].astype(BF16)
                ko = pl.multiple_of(r * lk, HALF)
                kd[pl.ds(ko, lk), :] = kn[pl.ds(kbase + r, lk, stride=dil), :].astype(BF16)
                vd[pl.ds(ko, lk), :] = vn[pl.ds(kbase + r, lk, stride=dil), :].astype(BF16)
                return carry
            lax.fori_loop(0, dil, regroup, 0)

        shift = nbq.bit_length() - 1

        def band(idx, carry, di=di, dil=dil, nbq=nbq, shift=shift):
            r = idx >> shift
            b = idx & (nbq - 1)
            qo = pl.multiple_of(idx * HALF, HALF)
            ko = pl.multiple_of(idx * HALF + r * (2 * HALF), HALF)
            qb = qd[pl.ds(qo, HALF), :]
            kb = kd[pl.ds(ko, 3 * HALF), :]
            vb = vd[pl.ds(ko, 3 * HALF), :]
            lo = jnp.where(jnp.logical_and(b == 0, has_prev == 0), HALF, 0)
            hi = jnp.where(jnp.logical_and(b == nbq - 1, has_next == 0), 2 * HALF, 3 * HALF)
            edge = jnp.where(jnp.logical_and(col >= lo, col < hi), 0.0, NEG_INF)
            outs = []
            for h in range(2):
                qh = jnp.where(m0_q if h == 0 else jnp.logical_not(m0_q), qb, jnp.zeros_like(qb))
                s = _dot_nt(qh, kb) + (bias_ref[h * 3 + di] + edge)
                m = jnp.max(s, axis=-1, keepdims=True)
                p = jnp.exp(s - m)
                l = jnp.sum(p, axis=-1, keepdims=True)
                o = _dot(p.astype(BF16), vb)
                outs.append((o, m, l))
            o2 = jnp.where(m0_o, outs[0][0], outs[1][0])
            m2 = jnp.where(m0_o, outs[0][1], outs[1][1])
            l2 = jnp.where(m0_o, outs[0][2], outs[1][2])
            pos0 = r + (dil * HALF) * b
            if dil == 1:
                dst = pl.ds(pl.multiple_of(pos0, HALF), HALF)
            else:
                dst = pl.ds(pos0, HALF, stride=dil)
            acc_o[di][dst, :] = o2
            acc_m[di][dst, :] = m2
            acc_l[di][dst, :] = l2
            return carry

        lax.fori_loop(0, SEQ_BLOCK // HALF, band, 0)

    beta = beta_ref[...]

    def merge(t, carry):
        r0 = pl.multiple_of(t * rows, rows)
        sl = pl.ds(r0, rows)
        ms = [acc_m[d][sl, :] for d in range(3)]
        mx = jnp.maximum(jnp.maximum(ms[0], ms[1]), ms[2])
        num = jnp.zeros((rows, 128), F32)
        den = jnp.zeros((rows, 128), F32)
        for d in range(3):
            e = jnp.exp(ms[d] - mx)
            num = num + e * acc_o[d][sl, :]
            den = den + e * acc_l[d][sl, :]
        o_ref[sl, :] = ((num / den) * beta).astype(o_ref.dtype)
        return carry

    lax.fori_loop(0, SEQ_BLOCK // rows, merge, 0)


def _attn_bias_table():
    qi = np.arange(HALF)[:, None]
    kj = np.arange(3 * HALF)[None, :]
    rel = np.abs(kj - HALF - qi).astype(np.float32)
    n_heads = ATTN_WIDTH // HEAD_DIM
    slopes = 2.0 ** (-8.0 * (np.arange(n_heads, dtype=np.float32) + 1.0) / n_heads)
    tab = np.empty((n_heads, 3, HALF, 3 * HALF), np.float32)
    for h in range(n_heads):
        for di, dil in enumerate(DILATIONS):
            tab[h, di] = np.where(rel <= HALF, -(slopes[h] * (dil * rel)), NEG_INF)
    return jnp.asarray(tab.reshape(n_heads * 3, HALF, 3 * HALF))


def _attention(q, k, v, has_prev, has_next, q_norm_g, k_norm_g, attn_beta):
    ntok = q.shape[0]
    nb = ntok // SEQ_BLOCK
    hb = SEQ_BLOCK // HALO
    cur = lambda i, j, hp, hn: (i, j)
    prev = lambda i, j, hp, hn: (jnp.maximum(i * hb - 1, 0), j)
    nxt = lambda i, j, hp, hn: (jnp.minimum((i + 1) * hb, nb * hb - 1), j)
    vec = lambda i, j, hp, hn: (0, 0)
    blk = pl.BlockSpec((SEQ_BLOCK, 128), cur)
    halo_p = pl.BlockSpec((HALO, 128), prev)
    halo_n = pl.BlockSpec((HALO, 128), nxt)
    win = SEQ_BLOCK + 2 * HALO
    grid_spec = pltpu.PrefetchScalarGridSpec(
        num_scalar_prefetch=2,
        grid=(nb, N_PAIRS),
        in_specs=[blk, halo_p, blk, halo_n, halo_p, blk, halo_n,
                  pl.BlockSpec((1, 128), vec), pl.BlockSpec((1, 128), vec),
                  pl.BlockSpec((1, 128), lambda i, j, hp, hn: (0, j)),
                  pl.BlockSpec((6, HALF, 3 * HALF), lambda i, j, hp, hn: (j, 0, 0))],
        out_specs=blk,
        scratch_shapes=[pltpu.VMEM((SEQ_BLOCK, 128), F32), pltpu.VMEM((win, 128), F32), pltpu.VMEM((win, 128), F32),
                        pltpu.VMEM((SEQ_BLOCK, 128), BF16), pltpu.VMEM((win, 128), BF16), pltpu.VMEM((win, 128), BF16),
                        [pltpu.VMEM((SEQ_BLOCK, 128), F32)] * 3,
                        [pltpu.VMEM((SEQ_BLOCK, 128), F32)] * 3,
                        [pltpu.VMEM((SEQ_BLOCK, 128), F32)] * 3],
    )
    two = lambda g: jnp.tile(g.reshape(1, HEAD_DIM), (1, 2))
    return pl.pallas_call(
        _attn_kernel,
        grid_spec=grid_spec,
        out_shape=jax.ShapeDtypeStruct((ntok, ATTN_WIDTH), BF16),
        compiler_params=pltpu.CompilerParams(dimension_semantics=("arbitrary", "arbitrary"),
                                             vmem_limit_bytes=VMEM_LIMIT),
        name="attn",
    )(has_prev, has_next, q, k, k, k, v, v, v, two(q_norm_g), two(k_norm_g), attn_beta.reshape(1, -1),
      _attn_bias_table())


def _prep_kernel(first_ref, last_ref,
                 p_ref, hp_ref, hn_ref, mup_ref, mun_ref, w0_ref, wup_ref, a0_ref, aup_ref, gup_ref,
                 kk_ref, ka_ref, rk_ref, ones_ref,
                 r_out, kk_out, v_out, g_out, bonus_out, lw_out, kd_out, bd_out):
    i = pl.program_id(0)
    tm = p_ref.shape[0]
    p = p_ref[...]
    row = lax.broadcasted_iota(jnp.int32, (tm, 1), 0)
    keep_prev = jnp.where(first_ref[i] == 0, 1.0, 0.0)
    keep_next = jnp.where(last_ref[i] == 0, 1.0, 0.0)
    prev = jnp.where(row == 0, hp_ref[7:8, :] * keep_prev, pltpu.roll(p, 1, 0))
    nxt = jnp.where(row == tm - 1, hn_ref[0:1, :] * keep_next, pltpu.roll(p, tm - 1, 0))
    ps = p + mup_ref[...] * (prev - p) + mun_ref[...] * (nxt - p)

    c = RWKV_WIDTH
    r = ps[:, 0:c]
    k = ps[:, c:2 * c]
    v = ps[:, 2 * c:3 * c]
    wd = ps[:, 3 * c:3 * c + 128]
    ad = ps[:, 3 * c + 128:3 * c + 256]
    gd = ps[:, 3 * c + 256:3 * c + 384]

    w_raw = w0_ref[...] + _dot(jnp.tanh(wd).astype(BF16), wup_ref[...])
    lw = (-float(np.exp(-0.5))) * _sigmoid(w_raw)
    a = _sigmoid(a0_ref[...] + _dot(ad.astype(BF16), aup_ref[...]))
    g = _dot(_sigmoid(gd).astype(BF16), gup_ref[...])

    ones_bd = ones_ref[...]
    kkv = k * kk_ref[...]
    ssq = _dot((kkv * kkv).astype(BF16), ones_bd)
    kkv = kkv * lax.rsqrt(jnp.maximum(ssq, 1e-24))
    ka = ka_ref[...]
    kd0 = k * (1.0 + (a[:, 0:c] - 1.0) * ka)
    kd1 = k * (1.0 + (a[:, c:2 * c] - 1.0) * ka)
    hsum = _dot((r * (kd0 + kd1) * rk_ref[...]).astype(BF16), ones_bd)

    r_out[...] = r
    kk_out[...] = kkv
    v_out[...] = v
    g_out[...] = g
    bonus_out[...] = hsum * v
    lw_out[0] = lw[:, 0:c]
    lw_out[1] = lw[:, c:2 * c]
    kd_out[0] = kd0
    kd_out[1] = kd1
    bd_out[0] = kkv * a[:, 0:c]
    bd_out[1] = kkv * a[:, c:2 * c]


def _block_diag2(m):
    z = jnp.zeros_like(m[0])
    return jnp.concatenate([jnp.concatenate([m[0], z], axis=1), jnp.concatenate([z, m[1]], axis=1)], axis=0)


def _head_ones():
    idx = np.arange(RWKV_WIDTH) // HEAD_DIM
    return jnp.asarray((idx[:, None] == idx[None, :]).astype(np.float32), dtype=BF16)


def _prep(p_rw, first_tile, last_tile, mu_prev, mu_next, w0, w_up, a0, a_up, g_up, k_k, k_a, r_k):
    ntok = p_rw.shape[0]
    tm = PREP_TILE
    nt = ntok // tm
    c = RWKV_WIDTH
    row = lambda i, f, l: (i, 0)
    const = lambda i, f, l: (0, 0)
    grid_spec = pltpu.PrefetchScalarGridSpec(
        num_scalar_prefetch=2,
        grid=(nt,),
        in_specs=[pl.BlockSpec((tm, RWKV_IN), row),
                  pl.BlockSpec((8, RWKV_IN), lambda i, f, l: (jnp.maximum(i * (tm // 8) - 1, 0), 0)),
                  pl.BlockSpec((8, RWKV_IN), lambda i, f, l: (jnp.minimum((i + 1) * (tm // 8), ntok // 8 - 1), 0)),
                  pl.BlockSpec((1, RWKV_IN), const), pl.BlockSpec((1, RWKV_IN), const),
                  pl.BlockSpec((1, 2 * c), const), pl.BlockSpec((128, 2 * c), const),
                  pl.BlockSpec((1, 2 * c), const), pl.BlockSpec((128, 2 * c), const),
                  pl.BlockSpec((128, c), const),
                  pl.BlockSpec((1, c), const), pl.BlockSpec((1, c), const), pl.BlockSpec((1, c), const),
                  pl.BlockSpec((c, c), const)],
        out_specs=[pl.BlockSpec((tm, c), row)] * 5 + [pl.BlockSpec((2, tm, c), lambda i, f, l: (0, i, 0))] * 3,
    )
    tok = jax.ShapeDtypeStruct((ntok, c), F32)
    tok2 = jax.ShapeDtypeStruct((2, ntok, c), F32)
    return pl.pallas_call(
        _prep_kernel,
        grid_spec=grid_spec,
        out_shape=[tok] * 5 + [tok2] * 3,
        compiler_params=pltpu.CompilerParams(dimension_semantics=("arbitrary",), vmem_limit_bytes=VMEM_LIMIT),
        name="rwkv_prep",
    )(first_tile, last_tile, p_rw, p_rw, p_rw, mu_prev.reshape(1, -1), mu_next.reshape(1, -1),
      w0.reshape(1, -1), _block_diag2(w_up).astype(BF16), a0.reshape(1, -1), _block_diag2(a_up).astype(BF16),
      g_up.astype(BF16), k_k.reshape(1, -1), k_a.reshape(1, -1), r_k.reshape(1, -1), _head_ones())


def _stack_heads(x):
    m0 = _lane_is_head0(x.shape)
    z = jnp.zeros_like(x)
    return jnp.concatenate([jnp.where(m0, x, z), jnp.where(m0, z, x)], axis=0)


def _scan_chunk(lw, r, kk, v, kd, bd, z, rev):
    c = CHUNK
    rowi = lax.broadcasted_iota(jnp.int32, (c, 128), 0)
    colj = lax.broadcasted_iota(jnp.int32, (c, 128), 1) & (c - 1)
    ti = lax.broadcasted_iota(jnp.int32, (c, c), 0)
    tj = lax.broadcasted_iota(jnp.int32, (c, c), 1)
    if rev:
        strict, incl, tri = colj > rowi, colj >= rowi, tj >= ti
    else:
        strict, incl, tri = colj < rowi, colj <= rowi, tj <= ti
    eye2 = (colj == rowi).astype(F32)
    tri = tri.astype(BF16)

    hi, mid, lo = _split3(lw)
    cl3 = _dot(tri, jnp.concatenate([hi, mid, lo], axis=1))
    cl = cl3[:, 0:128] + cl3[:, 128:256] + cl3[:, 256:384]
    g_in = jnp.exp(cl)
    g_ex = jnp.exp(cl - lw)
    g_inv = jnp.exp(-cl)
    g_tot = jnp.exp(cl[0:1, :] if rev else cl[c - 1:c, :])

    rt = r * g_in
    bt = kk * g_ex
    at = -(bd * g_inv)
    kt = kd * g_inv
    br = jnp.concatenate([bt, rt], axis=0).astype(BF16)
    ak = jnp.concatenate([_stack_heads(at), _stack_heads(kt)], axis=0).astype(BF16)
    lm = _dot_nt(br, ak)
    la = jnp.where(strict, lm[0:c, 0:128], 0.0)
    lk = jnp.where(strict, lm[0:c, 128:256], 0.0)
    ma = jnp.where(incl, lm[c:2 * c, 0:128], 0.0)
    mk = jnp.where(incl, lm[c:2 * c, 128:256], 0.0)

    t = eye2 + la
    pw = la
    for _ in range(5):
        pw = _dot(pw.astype(BF16), _stack_heads(pw).astype(BF16))
        t = t + _dot(t.astype(BF16), _stack_heads(pw).astype(BF16))

    sv = _stack_heads(v).astype(BF16)
    brz = _dot_nt(br, z.astype(BF16))
    x = brz[0:c] + _dot(lk.astype(BF16), sv)
    u = _dot(t.astype(BF16), _stack_heads(x).astype(BF16))
    y = brz[c:2 * c] + _dot(jnp.concatenate([ma, mk], axis=1).astype(BF16),
                            jnp.concatenate([_stack_heads(u).astype(BF16), sv], axis=0))
    uv = jnp.concatenate([u, v], axis=0).astype(BF16)
    akg = jnp.concatenate([at * g_tot, kt * g_tot], axis=0).astype(BF16)
    zr = lax.broadcasted_iota(jnp.int32, (128, 128), 0) < HEAD_DIM
    zc = lax.broadcasted_iota(jnp.int32, (128, 128), 1) < HEAD_DIM
    z_new = z * g_tot + jnp.where(zr == zc, _dot_tn(uv, akg), 0.0)
    return y, z_new


def _scan_kernel(bmap_ref, reset_ref,
                 rf, kkf, vf, lwf, kdf, bdf, rb, kkb, vb, lwb, kdb, bdb,
                 yf_ref, yb_ref, zf, zb):
    del bmap_ref
    g = pl.program_id(0)

    @pl.when(reset_ref[g] == 1)
    def _():
        zf[...] = jnp.zeros_like(zf)
        zb[...] = jnp.zeros_like(zb)

    nchunk = SCAN_BLOCK // CHUNK

    def body(ci, carry):
        for rev, refs, y_ref, z_ref in ((False, (lwf, rf, kkf, vf, kdf, bdf), yf_ref, zf),
                                        (True, (lwb, rb, kkb, vb, kdb, bdb), yb_ref, zb)):
            cc = (nchunk - 1 - ci) if rev else ci
            rows = pl.ds(pl.multiple_of(cc * CHUNK, CHUNK), CHUNK)
            for pr in range(N_PAIRS):
                lanes = slice(128 * pr, 128 * (pr + 1))
                args = [ref[rows, lanes] for ref in refs]
                y, z_new = _scan_chunk(*args, z_ref[pr], rev)
                y_ref[rows, lanes] = y
                z_ref[pr] = z_new
        return carry

    lax.fori_loop(0, nchunk, body, 0)


def _scan(r, kk, v, lw, kd, bd, bmap_bwd, reset):
    ntok = r.shape[0]
    c = RWKV_WIDTH
    ns = ntok // SCAN_BLOCK
    fwd = lambda g, bm, rs: (g, 0)
    bwd = lambda g, bm, rs: (bm[g], 0)
    fwd2 = lambda g, bm, rs: (0, g, 0)
    bwd2 = lambda g, bm, rs: (1, bm[g], 0)
    tokf = pl.BlockSpec((SCAN_BLOCK, c), fwd)
    tokb = pl.BlockSpec((SCAN_BLOCK, c), bwd)
    dirf = pl.BlockSpec((None, SCAN_BLOCK, c), fwd2)
    dirb = pl.BlockSpec((None, SCAN_BLOCK, c), bwd2)
    grid_spec = pltpu.PrefetchScalarGridSpec(
        num_scalar_prefetch=2,
        grid=(ns,),
        in_specs=[tokf, tokf, tokf, dirf, dirf, dirf, tokb, tokb, tokb, dirb, dirb, dirb],
        out_specs=[tokf, tokb],
        scratch_shapes=[pltpu.VMEM((N_PAIRS, 128, 128), F32), pltpu.VMEM((N_PAIRS, 128, 128), F32)],
    )
    tok = jax.ShapeDtypeStruct((ntok, c), F32)
    return pl.pallas_call(
        _scan_kernel,
        grid_spec=grid_spec,
        out_shape=[tok, tok],
        compiler_params=pltpu.CompilerParams(dimension_semantics=("arbitrary",), vmem_limit_bytes=VMEM_LIMIT),
        name="rwkv_scan",
    )(bmap_bwd, reset, r, kk, v, lw, kd, bd, r, kk, v, lw, kd, bd)


def _post_kernel(mrow_ref, x_ref, yf_ref, yb_ref, bonus_ref, g_ref, attn_ref, gt_ref, sh_ref, sc_ref,
                 g2_ref, lnw_ref, lnb_ref, ones_ref, wo_ref, x1_ref, h2_ref):
    del mrow_ref
    y = yf_ref[...] + yb_ref[...]
    ones_bd = ones_ref[...]
    y1, y2, _ = _split3(y)
    mu = (_dot(y1, ones_bd) + _dot(y2, ones_bd)) * (1.0 / HEAD_DIM)
    d = y - mu
    var = _dot((d * d).astype(BF16), ones_bd) * (1.0 / HEAD_DIM)
    yn = d * lax.rsqrt(var + LN_X_EPS) * lnw_ref[...] + lnb_ref[...]
    rw = ((yn + bonus_ref[...]) * g_ref[...]).astype(BF16)
    a = ATTN_WIDTH
    mix = _dot(attn_ref[...], wo_ref[0:a, :]) + _dot(rw, wo_ref[a:, :])
    x1 = x_ref[...] + gt_ref[0] * mix
    x1_ref[...] = x1
    h2_ref[...] = _modulated_rmsnorm(x1, g2_ref[...], sc_ref[0], sh_ref[0]).astype(BF16)


def _post(x, yf, yb, bonus, g, attn, mod3, mrow_tile, g_norm2, ln_x_w, ln_x_b, w_out_bf):
    ntok = x.shape[0]
    nt = ntok // ROW_TILE
    c = RWKV_WIDTH
    row = lambda i, mr: (i, 0)
    const = lambda i, mr: (0, 0)
    modspec = lambda j: pl.BlockSpec((1, 1, D_MODEL), lambda i, mr: (mr[i] * 6 + j, 0, 0))
    wide = pl.BlockSpec((ROW_TILE, D_MODEL), row)
    half = pl.BlockSpec((ROW_TILE, c), row)
    grid_spec = pltpu.PrefetchScalarGridSpec(
        num_scalar_prefetch=1,
        grid=(nt,),
        in_specs=[wide, half, half, half, half, half, modspec(2), modspec(3), modspec(4),
                  pl.BlockSpec((1, D_MODEL), const), pl.BlockSpec((1, c), const), pl.BlockSpec((1, c), const),
                  pl.BlockSpec((c, c), const), pl.BlockSpec((D_MODEL, D_MODEL), const)],
        out_specs=[wide, wide],
    )
    return pl.pallas_call(
        _post_kernel,
        grid_spec=grid_spec,
        out_shape=[jax.ShapeDtypeStruct((ntok, D_MODEL), F32), jax.ShapeDtypeStruct((ntok, D_MODEL), BF16)],
        compiler_params=pltpu.CompilerParams(dimension_semantics=("arbitrary",), vmem_limit_bytes=VMEM_LIMIT),
        name="post",
    )(mrow_tile, x, yf, yb, bonus, g, attn, mod3, mod3, mod3, g_norm2.reshape(1, -1),
      ln_x_w.reshape(1, -1), ln_x_b.reshape(1, -1), _head_ones(), w_out_bf)


def _ffn_kernel(mrow_ref, x1_ref, h2_ref, gt_ref, w1_ref, w2_ref, o_ref):
    del mrow_ref
    h = h2_ref[...]
    acc = jnp.zeros((h.shape[0], D_MODEL), F32)
    step = 1024
    for j in range(D_FF // step):
        a = jnp.maximum(_dot(h, w1_ref[:, j * step:(j + 1) * step]), 0.0)
        acc = acc + _dot((a * a).astype(BF16), w2_ref[j * step:(j + 1) * step, :])
    o_ref[...] = x1_ref[...] + gt_ref[0] * acc


def _ffn(x1, h2, mod3, mrow_tile, w1_bf, w2_bf):
    ntok = x1.shape[0]
    nt = ntok // ROW_TILE
    row = lambda i, mr: (i, 0)
    const = lambda i, mr: (0, 0)
    wide = pl.BlockSpec((ROW_TILE, D_MODEL), row)
    grid_spec = pltpu.PrefetchScalarGridSpec(
        num_scalar_prefetch=1,
        grid=(nt,),
        in_specs=[wide, wide, pl.BlockSpec((1, 1, D_MODEL), lambda i, mr: (mr[i] * 6 + 5, 0, 0)),
                  pl.BlockSpec((D_MODEL, D_FF), const), pl.BlockSpec((D_FF, D_MODEL), const)],
        out_specs=wide,
    )
    return pl.pallas_call(
        _ffn_kernel,
        grid_spec=grid_spec,
        out_shape=jax.ShapeDtypeStruct((ntok, D_MODEL), F32),
        compiler_params=pltpu.CompilerParams(dimension_semantics=("arbitrary",), vmem_limit_bytes=VMEM_LIMIT),
        name="ffn",
    )(mrow_tile, x1, h2, mod3, w1_bf, w2_bf)


def _sequence_tables(seq_blocks):
    mod_row, has_prev, has_next = [], [], []
    for s, n in enumerate(seq_blocks):
        for b in range(n):
            mod_row.append(s)
            has_prev.append(int(b > 0))
            has_next.append(int(b < n - 1))
    per = SEQ_BLOCK // SCAN_BLOCK
    bmap_bwd, reset = [], []
    start = 0
    for n in seq_blocks:
        ns = n * per
        for t in range(ns):
            bmap_bwd.append(start + ns - 1 - t)
            reset.append(int(t == 0))
        start += ns
    i32 = lambda a: jnp.asarray(np.asarray(a, np.int32))
    rep = lambda a, k: np.repeat(np.asarray(a, np.int32), k)
    row_tiles = SEQ_BLOCK // ROW_TILE
    prep_tiles = SEQ_BLOCK // PREP_TILE
    first = np.zeros(len(mod_row) * prep_tiles, np.int32)
    last = np.zeros(len(mod_row) * prep_tiles, np.int32)
    for b in range(len(mod_row)):
        if not has_prev[b]:
            first[b * prep_tiles] = 1
        if not has_next[b]:
            last[(b + 1) * prep_tiles - 1] = 1
    return dict(mrow_tile=i32(rep(mod_row, row_tiles)), has_prev=i32(has_prev), has_next=i32(has_next),
                bmap_bwd=i32(bmap_bwd), reset=i32(reset), first=i32(first), last=i32(last))


def _layer(x, c_all, tabs, w_ada, b_ada, g_norm1, g_norm2, w_in, q_norm_g, k_norm_g, attn_beta,
           mu_prev, mu_next, w0, w_up, a0, a_up, g_up, k_k, k_a, r_k, ln_x_w, ln_x_b, w_out, w_ff1, w_ff2):
    mod = _ada(c_all, w_ada, b_ada)
    mod3 = mod.reshape(-1, 1, D_MODEL)
    q, k, v, p_rw = _inproj(x, mod3, tabs["mrow_tile"], g_norm1, w_in.astype(BF16))
    attn = _attention(q, k, v, tabs["has_prev"], tabs["has_next"], q_norm_g, k_norm_g, attn_beta)
    r, kk, vv, g, bonus, lw, kd, bd = _prep(p_rw, tabs["first"], tabs["last"], mu_prev, mu_next, w0, w_up,
                                            a0, a_up, g_up, k_k, k_a, r_k)
    yf, yb = _scan(r, kk, vv, lw, kd, bd, tabs["bmap_bwd"], tabs["reset"])
    x1, h2 = _post(x, yf, yb, bonus, g, attn, mod3, tabs["mrow_tile"], g_norm2, ln_x_w, ln_x_b,
                   w_out.astype(BF16))
    return _ffn(x1, h2, mod3, tabs["mrow_tile"], w_ff1.astype(BF16), w_ff2.astype(BF16))


def kernel(x_prompt, x_sample, c_prompt, c_sample, w_ada, b_ada, g_norm1, g_norm2, w_in, q_norm_g, k_norm_g, attn_beta, mu_prev, mu_next, w0, w_up, a0, a_up, g_up, k_k, k_a, r_k, ln_x_w, ln_x_b, w_out, w_ff1, w_ff2):
    bp, sp, d = x_prompt.shape
    bs, ss, _ = x_sample.shape
    assert d == D_MODEL and sp % SEQ_BLOCK == 0 and ss % SEQ_BLOCK == 0
    seq_blocks = [sp // SEQ_BLOCK] * bp + [ss // SEQ_BLOCK] * bs
    tabs = _sequence_tables(seq_blocks)
    x = jnp.concatenate([x_prompt.reshape(bp * sp, d), x_sample.reshape(bs * ss, d)], axis=0)
    n_seq = bp + bs
    pad = (-n_seq) % 8
    c_all = jnp.concatenate([c_prompt, c_sample, jnp.zeros((pad, d), F32)], axis=0)
    y = x
    for i in range(w_ada.shape[0]):
        y = _layer(y, c_all, tabs, w_ada[i], b_ada[i], g_norm1[i], g_norm2[i], w_in[i], q_norm_g[i], k_norm_g[i],
                   attn_beta[i], mu_prev[i], mu_next[i], w0[i], w_up[i], a0[i], a_up[i], g_up[i], k_k[i], k_a[i],
                   r_k[i], ln_x_w[i], ln_x_b[i], w_out[i], w_ff1[i], w_ff2[i])
    return (y[:bp * sp].reshape(bp, sp, d), y[bp * sp:].reshape(bs, ss, d))
```

```python
import functools

import numpy as np
import jax
import jax.numpy as jnp
from jax import lax
from jax.experimental import pallas as pl
from jax.experimental.pallas import tpu as pltpu

F32 = jnp.float32
BF16 = jnp.bfloat16

D_MODEL = 1024
HEAD_DIM = 64
ATTN_WIDTH = 512
RWKV_WIDTH = 512
N_PAIRS = ATTN_WIDTH // 128
RWKV_IN = 1920
IN_WIDTH = 3 * ATTN_WIDTH + RWKV_IN
D_FF = 4096
DILATIONS = (1, 4, 16)
HALF = 64
NORM_EPS = 1e-6
LN_X_EPS = 64e-5
NEG_INF = -1e30

SEQ_BLOCK = 2048
HALO = 1024
SCAN_BLOCK = 512
CHUNK = 64
ROW_TILE = 512
PREP_TILE = 256
BANDS_PER_STEP = 4

VMEM_LIMIT = 56 * 1024 * 1024


def _dot(a, b):
    return jnp.dot(a, b, preferred_element_type=F32)


def _dot_nt(a, b):
    return lax.dot_general(a, b, (((1,), (1,)), ((), ())), preferred_element_type=F32)


def _dot_tn(a, b):
    return lax.dot_general(a, b, (((0,), (0,)), ((), ())), preferred_element_type=F32)


def _split3(x):
    hi = x.astype(BF16)
    r1 = x - hi.astype(F32)
    mid = r1.astype(BF16)
    lo = (r1 - mid.astype(F32)).astype(BF16)
    return hi, mid, lo


def _sigmoid(x):
    return 1.0 / (1.0 + jnp.exp(-x))


def _ada_kernel(c_ref, w_ref, b_ref, o_ref):
    c = c_ref[...]
    s = c * _sigmoid(c)
    s1, s2, _ = _split3(s)
    w = w_ref[...]
    w1, w2, _ = _split3(w)
    o_ref[...] = _dot(s1, w1) + (_dot(s1, w2) + _dot(s2, w1)) + b_ref[...]


def _ada(c_all, w_ada, b_ada):
    n = c_all.shape[0]
    nt = w_ada.shape[1] // D_MODEL
    return pl.pallas_call(
        _ada_kernel,
        grid=(nt,),
        in_specs=[pl.BlockSpec((n, D_MODEL), lambda j: (0, 0)),
                  pl.BlockSpec((D_MODEL, D_MODEL), lambda j: (0, j)),
                  pl.BlockSpec((1, D_MODEL), lambda j: (0, j))],
        out_specs=pl.BlockSpec((n, D_MODEL), lambda j: (0, j)),
        out_shape=jax.ShapeDtypeStruct((n, w_ada.shape[1]), F32),
        compiler_params=pltpu.CompilerParams(dimension_semantics=("arbitrary",), vmem_limit_bytes=VMEM_LIMIT),
        name="ada",
    )(c_all, w_ada, b_ada.reshape(1, -1))


def _modulated_rmsnorm(x, g, scale, shift):
    ms = jnp.mean(x * x, axis=-1, keepdims=True)
    return (x * lax.rsqrt(ms + NORM_EPS) * g) * (1.0 + scale) + shift


def _inproj_kernel(mrow_ref, x_ref, g_ref, sh_ref, sc_ref, w_ref, q_ref, k_ref, v_ref, p_ref):
    del mrow_ref
    h = _modulated_rmsnorm(x_ref[...], g_ref[...], sc_ref[0], sh_ref[0]).astype(BF16)
    a = ATTN_WIDTH
    q_ref[...] = _dot(h, w_ref[:, 0:a])
    k_ref[...] = _dot(h, w_ref[:, a:2 * a])
    v_ref[...] = _dot(h, w_ref[:, 2 * a:3 * a])
    p_ref[...] = _dot(h, w_ref[:, 3 * a:])


def _inproj(x, mod3, mrow_tile, g_norm1, w_in_bf):
    ntok = x.shape[0]
    nt = ntok // ROW_TILE
    row = lambda i, mr: (i, 0)
    const = lambda i, mr: (0, 0)
    grid_spec = pltpu.PrefetchScalarGridSpec(
        num_scalar_prefetch=1,
        grid=(nt,),
        in_specs=[pl.BlockSpec((ROW_TILE, D_MODEL), row),
                  pl.BlockSpec((1, D_MODEL), const),
                  pl.BlockSpec((1, 1, D_MODEL), lambda i, mr: (mr[i] * 6 + 0, 0, 0)),
                  pl.BlockSpec((1, 1, D_MODEL), lambda i, mr: (mr[i] * 6 + 1, 0, 0)),
                  pl.BlockSpec((D_MODEL, IN_WIDTH), const)],
        out_specs=[pl.BlockSpec((ROW_TILE, ATTN_WIDTH), row)] * 3 + [pl.BlockSpec((ROW_TILE, RWKV_IN), row)],
    )
    return pl.pallas_call(
        _inproj_kernel,
        grid_spec=grid_spec,
        out_shape=[jax.ShapeDtypeStruct((ntok, ATTN_WIDTH), F32)] * 3 + [jax.ShapeDtypeStruct((ntok, RWKV_IN), F32)],
        compiler_params=pltpu.CompilerParams(dimension_semantics=("arbitrary",), vmem_limit_bytes=VMEM_LIMIT),
        name="inproj",
    )(mrow_tile, x, g_norm1.reshape(1, -1), mod3, mod3, w_in_bf)


def _lane_is_head0(shape):
    return lax.broadcasted_iota(jnp.int32, shape, len(shape) - 1) < HEAD_DIM


def _head_rmsnorm(x, gain):
    m0 = _lane_is_head0(x.shape)
    x2 = x * x
    s0 = jnp.sum(jnp.where(m0, x2, 0.0), axis=-1, keepdims=True)
    s1 = jnp.sum(jnp.where(m0, 0.0, x2), axis=-1, keepdims=True)
    ms = jnp.where(m0, s0, s1) * (1.0 / HEAD_DIM)
    return x * lax.rsqrt(ms + NORM_EPS) * gain


def _attn_kernel(hasprev_ref, hasnext_ref,
                 q_ref, kp_ref, kc_ref, kx_ref, vp_ref, vc_ref, vx_ref, qg_ref, kg_ref, beta_ref, bias_ref,
                 o_ref,
                 qn, kn, vn, qd, kd, vd, acc_o, acc_m, acc_l):
    i = pl.program_id(0)
    has_prev = hasprev_ref[i]
    has_next = hasnext_ref[i]
    rows = 256

    def fill(src_ref, dst_ref, dst_off, nrows, fn):
        def body(t, carry):
            r0 = pl.multiple_of(t * rows, rows)
            dst_ref[pl.ds(dst_off + r0, rows), :] = fn(src_ref[pl.ds(r0, rows), :])
            return carry
        lax.fori_loop(0, nrows // rows, body, 0)

    qg = qg_ref[...]
    kg = kg_ref[...]
    fill(q_ref, qn, 0, SEQ_BLOCK, lambda x: _head_rmsnorm(x, qg) * (HEAD_DIM ** -0.5))
    fill(kp_ref, kn, 0, HALO, lambda x: _head_rmsnorm(x, kg))
    fill(kc_ref, kn, HALO, SEQ_BLOCK, lambda x: _head_rmsnorm(x, kg))
    fill(kx_ref, kn, HALO + SEQ_BLOCK, HALO, lambda x: _head_rmsnorm(x, kg))
    fill(vp_ref, vn, 0, HALO, lambda x: x)
    fill(vc_ref, vn, HALO, SEQ_BLOCK, lambda x: x)
    fill(vx_ref, vn, HALO + SEQ_BLOCK, HALO, lambda x: x)

    m0_o = _lane_is_head0((HALF, 128))
    col =lax.broadcasted_iota(jnp.int32, (1, 3 * HALF), 1)

    for di, dil in enumerate(DILATIONS):
        lq = SEQ_BLOCK // dil
        lk = lq + 2 * HALF
        nbq = lq // HALF
        kbase = HALO - HALF * dil

        if dil == 1:
            def cast_rows(src_ref, src_off, dst_ref, nrows, chunk):
                def body(t, carry):
                    r0 = pl.multiple_of(t * chunk, chunk)
                    dst_ref[pl.ds(r0, chunk), :] = src_ref[pl.ds(src_off + r0, chunk), :].astype(BF16)
                    return carry
                lax.fori_loop(0, nrows // chunk, body, 0)
            cast_rows(qn, 0, qd, lq, 256)
            cast_rows(kn, kbase, kd, lk, 128)
            cast_rows(vn, kbase, vd, lk, 128)
        else:
            def regroup(r, carry, dil=dil, lq=lq, lk=lk, kbase=kbase):
                qd[pl.ds(pl.multiple_of(r * lq, HALF), lq), :] = qn[pl.ds(r, lq, stride=dil), :].astype(BF16)
                ko = pl.multiple_of(r * lk, HALF)
                kd[pl.ds(ko, lk), :] = kn[pl.ds(kbase + r, lk, stride=dil), :].astype(BF16)
                vd[pl.ds(ko, lk), :] = vn[pl.ds(kbase + r, lk, stride=dil), :].astype(BF16)
                return carry
            lax.fori_loop(0, dil, regroup, 0)

        shift = nbq.bit_length() - 1

        def bands(it, carry, di=di, dil=dil, nbq=nbq, shift=shift):
            work = []
            for u in range(BANDS_PER_STEP):
                idx = it * BANDS_PER_STEP + u
                r = idx >> shift
                b = idx & (nbq - 1)
                qo = pl.multiple_of(idx * HALF, HALF)
                ko = pl.multiple_of(idx * HALF + r * (2 * HALF), HALF)
                lo = jnp.where(jnp.logical_and(b == 0, has_prev == 0), HALF, 0)
                hi = jnp.where(jnp.logical_and(b == nbq - 1, has_next == 0), 2 * HALF, 3 * HALF)
                edge = jnp.where(jnp.logical_and(col >= lo, col < hi), 0.0, NEG_INF)
                work.append((r + (dil * HALF) * b, edge, qd[pl.ds(qo, HALF), :],
                             kd[pl.ds(ko, 3 * HALF), :], vd[pl.ds(ko, 3 * HALF), :]))
            s = [_dot_nt(_stack_heads(w[2]), w[3]) for w in work]
            s = [si + (bias_ref[di] + w[1]) for si, w in zip(s, work)]
            m = [jnp.max(si, axis=-1, keepdims=True) for si in s]
            p = [jnp.exp(si - mi) for si, mi in zip(s, m)]
            l = [jnp.sum(pi, axis=-1, keepdims=True) for pi in p]
            o = [_dot(pi.astype(BF16), w[4]) for pi, w in zip(p, work)]
            for w, oi, mi, li in zip(work, o, m, l):
                if dil == 1:
                    dst = pl.ds(pl.multiple_of(w[0], HALF), HALF)
                else:
                    dst = pl.ds(w[0], HALF, stride=dil)
                acc_o[di][dst, :] = jnp.where(m0_o, oi[0:HALF], oi[HALF:2 * HALF])
                acc_m[di][dst, :] = jnp.where(m0_o, mi[0:HALF], mi[HALF:2 * HALF])
                acc_l[di][dst, :] = jnp.where(m0_o, li[0:HALF], li[HALF:2 * HALF])
            return carry

        lax.fori_loop(0, SEQ_BLOCK // HALF // BANDS_PER_STEP, bands, 0)

    beta = beta_ref[...]

    def merge(t, carry):
        r0 = pl.multiple_of(t * rows, rows)
        sl = pl.ds(r0, rows)
        ms = [acc_m[d][sl, :] for d in range(3)]
        mx = jnp.maximum(jnp.maximum(ms[0], ms[1]), ms[2])
        num = jnp.zeros((rows, 128), F32)
        den = jnp.zeros((rows, 128), F32)
        for d in range(3):
            e = jnp.exp(ms[d] - mx)
            num = num + e * acc_o[d][sl, :]
            den = den + e * acc_l[d][sl, :]
        o_ref[sl, :] = ((num / den) * beta).astype(o_ref.dtype)
        return carry

    lax.fori_loop(0, SEQ_BLOCK // rows, merge, 0)


def _attn_bias_table():
    qi = np.arange(HALF)[:, None]
    kj = np.arange(3 * HALF)[None, :]
    rel = np.abs(kj - HALF - qi).astype(np.float32)
    n_heads = ATTN_WIDTH // HEAD_DIM
    slopes = 2.0 ** (-8.0 * (np.arange(n_heads, dtype=np.float32) + 1.0) / n_heads)
    tab = np.empty((N_PAIRS, 3, 2, HALF, 3 * HALF), np.float32)
    for h in range(n_heads):
        for di, dil in enumerate(DILATIONS):
            tab[h // 2, di, h % 2] = np.where(rel <= HALF, -(slopes[h] * (dil * rel)), NEG_INF)
    return jnp.asarray(tab.reshape(N_PAIRS * 3, 2 * HALF, 3 * HALF))


def _attention(q, k, v, has_prev, has_next, q_norm_g, k_norm_g, attn_beta):
    ntok = q.shape[0]
    nb = ntok // SEQ_BLOCK
    hb = SEQ_BLOCK // HALO
    cur = lambda i, j, hp, hn: (i, j)
    prev = lambda i, j, hp, hn: (jnp.maximum(i * hb - 1, 0), j)
    nxt = lambda i, j, hp, hn: (jnp.minimum((i + 1) * hb, nb * hb - 1), j)
    vec = lambda i, j, hp, hn: (0, 0)
    blk = pl.BlockSpec((SEQ_BLOCK, 128), cur)
    halo_p = pl.BlockSpec((HALO, 128), prev)
    halo_n = pl.BlockSpec((HALO, 128), nxt)
    win = SEQ_BLOCK + 2 * HALO
    grid_spec = pltpu.PrefetchScalarGridSpec(
        num_scalar_prefetch=2,
        grid=(nb, N_PAIRS),
        in_specs=[blk, halo_p, blk, halo_n, halo_p, blk, halo_n,
                  pl.BlockSpec((1, 128), vec), pl.BlockSpec((1, 128), vec),
                  pl.BlockSpec((1, 128), lambda i, j, hp, hn: (0, j)),
                  pl.BlockSpec((3, 2 * HALF, 3 * HALF), lambda i, j, hp, hn: (j, 0, 0))],
        out_specs=blk,
        scratch_shapes=[pltpu.VMEM((SEQ_BLOCK, 128), F32), pltpu.VMEM((win, 128), F32), pltpu.VMEM((win, 128), F32),
                        pltpu.VMEM((SEQ_BLOCK, 128), BF16), pltpu.VMEM((win, 128), BF16), pltpu.VMEM((win, 128), BF16),
                        [pltpu.VMEM((SEQ_BLOCK, 128), F32)] * 3,
                        [pltpu.VMEM((SEQ_BLOCK, 128), F32)] * 3,
                        [pltpu.VMEM((SEQ_BLOCK, 128), F32)] * 3],
    )
    two = lambda g: jnp.tile(g.reshape(1, HEAD_DIM), (1, 2))
    return pl.pallas_call(
        _attn_kernel,
        grid_spec=grid_spec,
        out_shape=jax.ShapeDtypeStruct((ntok, ATTN_WIDTH), BF16),
        compiler_params=pltpu.CompilerParams(dimension_semantics=("arbitrary", "arbitrary"),
                                             vmem_limit_bytes=VMEM_LIMIT),
        name="attn",
    )(has_prev, has_next, q, k, k, k, v, v, v, two(q_norm_g), two(k_norm_g), attn_beta.reshape(1, -1),
      _attn_bias_table())


def _prep_kernel(first_ref, last_ref,
                 p_ref, hp_ref, hn_ref, mup_ref, mun_ref, w0_ref, wup_ref, a0_ref, aup_ref, gup_ref,
                 kk_ref, ka_ref, rk_ref, ones_ref,
                 r_out, kk_out, v_out, g_out, bonus_out, lw_out, kd_out, bd_out):
    i = pl.program_id(0)
    tm = p_ref.shape[0]
    p = p_ref[...]
    row = lax.broadcasted_iota(jnp.int32, (tm, 1), 0)
    keep_prev = jnp.where(first_ref[i] == 0, 1.0, 0.0)
    keep_next = jnp.where(last_ref[i] == 0, 1.0, 0.0)
    prev = jnp.where(row == 0, hp_ref[7:8, :] * keep_prev, pltpu.roll(p, 1, 0))
    nxt = jnp.where(row == tm - 1, hn_ref[0:1, :] * keep_next, pltpu.roll(p, tm - 1, 0))
    ps = p + mup_ref[...] * (prev - p) + mun_ref[...] * (nxt - p)

    c = RWKV_WIDTH
    r = ps[:, 0:c]
    k = ps[:, c:2 * c]
    v = ps[:, 2 * c:3 * c]
    wd = ps[:, 3 * c:3 * c + 128]
    ad = ps[:, 3 * c + 128:3 * c + 256]
    gd = ps[:, 3 * c + 256:3 * c + 384]

    w_raw = w0_ref[...] + _dot(jnp.tanh(wd).astype(BF16), wup_ref[...])
    lw = (-float(np.exp(-0.5))) * _sigmoid(w_raw)
    a = _sigmoid(a0_ref[...] + _dot(ad.astype(BF16), aup_ref[...]))
    g = _dot(_sigmoid(gd).astype(BF16), gup_ref[...])

    ones_bd = ones_ref[...]
    kkv = k * kk_ref[...]
    ssq = _dot((kkv * kkv).astype(BF16), ones_bd)
    kkv = kkv * lax.rsqrt(jnp.maximum(ssq, 1e-24))
    ka = ka_ref[...]
    kd0 = k * (1.0 + (a[:, 0:c] - 1.0) * ka)
    kd1 = k * (1.0 + (a[:, c:2 * c] - 1.0) * ka)
    hsum = _dot((r * (kd0 + kd1) * rk_ref[...]).astype(BF16), ones_bd)

    r_out[...] = r
    kk_out[...] = kkv
    v_out[...] = v
    g_out[...] = g
    bonus_out[...] = hsum * v
    lw_out[0] = lw[:, 0:c]
    lw_out[1] = lw[:, c:2 * c]
    kd_out[0] = kd0
    kd_out[1] = kd1
    bd_out[0] = kkv * a[:, 0:c]
    bd_out[1] = kkv * a[:, c:2 * c]


def _block_diag2(m):
    z = jnp.zeros_like(m[0])
    return jnp.concatenate([jnp.concatenate([m[0], z], axis=1), jnp.concatenate([z, m[1]], axis=1)], axis=0)


def _head_ones():
    idx = np.arange(RWKV_WIDTH) // HEAD_DIM
    return jnp.asarray((idx[:, None] == idx[None, :]).astype(np.float32), dtype=BF16)


def _prep(p_rw, first_tile, last_tile, mu_prev, mu_next, w0, w_up, a0, a_up, g_up, k_k, k_a, r_k):
    ntok = p_rw.shape[0]
    tm = PREP_TILE
    nt = ntok // tm
    c = RWKV_WIDTH
    row = lambda i, f, l: (i, 0)
    const = lambda i, f, l: (0, 0)
    grid_spec = pltpu.PrefetchScalarGridSpec(
        num_scalar_prefetch=2,
        grid=(nt,),
        in_specs=[pl.BlockSpec((tm, RWKV_IN), row),
                  pl.BlockSpec((8, RWKV_IN), lambda i, f, l: (jnp.maximum(i * (tm // 8) - 1, 0), 0)),
                  pl.BlockSpec((8, RWKV_IN), lambda i, f, l: (jnp.minimum((i + 1) * (tm // 8), ntok // 8 - 1), 0)),
                  pl.BlockSpec((1, RWKV_IN), const), pl.BlockSpec((1, RWKV_IN), const),
                  pl.BlockSpec((1, 2 * c), const), pl.BlockSpec((128, 2 * c), const),
                  pl.BlockSpec((1, 2 * c), const), pl.BlockSpec((128, 2 * c), const),
                  pl.BlockSpec((128, c), const),
                  pl.BlockSpec((1, c), const), pl.BlockSpec((1, c), const), pl.BlockSpec((1, c), const),
                  pl.BlockSpec((c, c), const)],
        out_specs=[pl.BlockSpec((tm, c), row)] * 5 + [pl.BlockSpec((2, tm, c), lambda i, f, l: (0, i, 0))] * 3,
    )
    tok = jax.ShapeDtypeStruct((ntok, c), F32)
    tok2 = jax.ShapeDtypeStruct((2, ntok, c), F32)
    return pl.pallas_call(
        _prep_kernel,
        grid_spec=grid_spec,
        out_shape=[tok] * 5 + [tok2] * 3,
        compiler_params=pltpu.CompilerParams(dimension_semantics=("arbitrary",), vmem_limit_bytes=VMEM_LIMIT),
        name="rwkv_prep",
    )(first_tile, last_tile, p_rw, p_rw, p_rw, mu_prev.reshape(1, -1), mu_next.reshape(1, -1),
      w0.reshape(1, -1), _block_diag2(w_up).astype(BF16), a0.reshape(1, -1), _block_diag2(a_up).astype(BF16),
      g_up.astype(BF16), k_k.reshape(1, -1), k_a.reshape(1, -1), r_k.reshape(1, -1), _head_ones())


def _stack_heads(x):
    m0 = _lane_is_head0(x.shape)
    z = jnp.zeros_like(x)
    return jnp.concatenate([jnp.where(m0, x, z), jnp.where(m0, z, x)], axis=0)


def _stack_heads_bf16(x):
    return _stack_heads(x.astype(BF16))


def _scan_masks(rev):
    c = CHUNK
    rowi = lax.broadcasted_iota(jnp.int32, (c, 128), 0)
    colj = lax.broadcasted_iota(jnp.int32, (c, 128), 1) & (c - 1)
    ti = lax.broadcasted_iota(jnp.int32, (c, c), 0)
    tj = lax.broadcasted_iota(jnp.int32, (c, c), 1)
    if rev:
        strict, incl, tri = colj > rowi, colj >= rowi, tj >= ti
    else:
        strict, incl, tri = colj < rowi, colj <= rowi, tj <= ti
    return dict(strict=strict, incl=incl, tri=tri.astype(BF16), eye2=(colj == rowi).astype(F32))


def _scan_chunks(ins, zs, masks):
    c = CHUNK
    n = len(ins)
    idx = range(n)
    lw = [a[0] for a in ins]
    v = [a[3] for a in ins]

    cl = []
    for i in idx:
        hi, mid, lo = _split3(lw[i])
        cl3 = _dot(masks[i]["tri"], jnp.concatenate([hi, mid, lo], axis=1))
        cl.append(cl3[:, 0:128] + cl3[:, 128:256] + cl3[:, 256:384])
    g_tot = [jnp.exp(cl[i][0:1, :] if masks[i]["rev"] else cl[i][c - 1:c, :]) for i in idx]
    g_inv = [jnp.exp(-cl[i]) for i in idx]
    rt = [ins[i][1] * jnp.exp(cl[i]) for i in idx]
    bt = [ins[i][2] * jnp.exp(cl[i] - lw[i]) for i in idx]
    at = [-(ins[i][5] * g_inv[i]) for i in idx]
    kt = [ins[i][4] * g_inv[i] for i in idx]
    br = [jnp.concatenate([bt[i], rt[i]], axis=0).astype(BF16) for i in idx]
    lm = [_dot_nt(br[i], jnp.concatenate([_stack_heads_bf16(at[i]), _stack_heads_bf16(kt[i])], axis=0))
          for i in idx]
    la = [jnp.where(masks[i]["strict"], lm[i][0:c, 0:128], 0.0) for i in idx]
    lk = [jnp.where(masks[i]["strict"], lm[i][0:c, 128:256], 0.0).astype(BF16) for i in idx]
    mam = [jnp.concatenate([jnp.where(masks[i]["incl"], lm[i][c:2 * c, 0:128], 0.0),
                            jnp.where(masks[i]["incl"], lm[i][c:2 * c, 128:256], 0.0)], axis=1).astype(BF16)
           for i in idx]

    sv = [_stack_heads_bf16(v[i]) for i in idx]
    brz = [_dot_nt(br[i], zs[i].astype(BF16)) for i in idx]
    x = [brz[i][0:c] + _dot(lk[i], sv[i]) for i in idx]

    t = [masks[i]["eye2"] + la[i] for i in idx]
    pw = [_dot(la[i].astype(BF16), _stack_heads_bf16(la[i])) for i in idx]
    for step in range(5):
        if step < 4:
            out = [_dot(pw[i].astype(BF16),
                        jnp.concatenate([_stack_heads_bf16(pw[i]), _stack_heads_bf16(t[i])], axis=1))
                   for i in idx]
            t = [t[i] + out[i][:, 128:256] for i in idx]
            pw = [out[i][:, 0:128] for i in idx]
        else:
            t = [t[i] + _dot(pw[i].astype(BF16), _stack_heads_bf16(t[i])) for i in idx]

    u = [_dot(t[i].astype(BF16), _stack_heads_bf16(x[i])) for i in idx]
    y = [brz[i][c:2 * c] + _dot(mam[i], jnp.concatenate([_stack_heads_bf16(u[i]), sv[i]], axis=0)) for i in idx]
    zr = lax.broadcasted_iota(jnp.int32, (128, 128), 0) < HEAD_DIM
    zc = lax.broadcasted_iota(jnp.int32, (128, 128), 1) < HEAD_DIM
    same_head = zr == zc
    z_new = []
    for i in idx:
        uv = jnp.concatenate([u[i], v[i]], axis=0).astype(BF16)
        akg = jnp.concatenate([at[i] * g_tot[i], kt[i] * g_tot[i]], axis=0).astype(BF16)
        z_new.append(zs[i] * g_tot[i] + jnp.where(same_head, _dot_tn(uv, akg), 0.0))
    return y, z_new


def _scan_kernel(bmap_ref, reset_ref,
                 rf, kkf, vf, lwf, kdf, bdf, rb, kkb, vb, lwb, kdb, bdb,
                 yf_ref, yb_ref, zf, zb):
    del bmap_ref
    g = pl.program_id(0)

    @pl.when(reset_ref[g] == 1)
    def _():
        zf[...] = jnp.zeros_like(zf)
        zb[...] = jnp.zeros_like(zb)

    nchunk = SCAN_BLOCK // CHUNK
    mask_f = dict(_scan_masks(False), rev=False)
    mask_b = dict(_scan_masks(True), rev=True)
    dirs = ((mask_f, (lwf, rf, kkf, vf, kdf, bdf), yf_ref, zf),
            (mask_b, (lwb, rb, kkb, vb, kdb, bdb), yb_ref, zb))

    def body(ci, carry):
        ins, zs, masks, outs = [], [], [], []
        for mask, refs, y_ref, z_ref in dirs:
            cc = (nchunk - 1 - ci) if mask["rev"] else ci
            rows = pl.ds(pl.multiple_of(cc * CHUNK, CHUNK), CHUNK)
            for pr in range(N_PAIRS):
                lanes = slice(128 * pr, 128 * (pr + 1))
                ins.append([ref[rows, lanes] for ref in refs])
                zs.append(z_ref[pr])
                masks.append(mask)
                outs.append((y_ref, z_ref, rows, lanes, pr))
        ys, z_new = _scan_chunks(ins, zs, masks)
        for (y_ref, z_ref, rows, lanes, pr), y, z in zip(outs, ys, z_new):
            y_ref[rows, lanes] = y
            z_ref[pr] = z
        return carry

    lax.fori_loop(0, nchunk, body, 0)


def _scan(r, kk, v, lw, kd, bd, bmap_bwd, reset):
    ntok = r.shape[0]
    c = RWKV_WIDTH
    ns = ntok // SCAN_BLOCK
    fwd = lambda g, bm, rs: (g, 0)
    bwd = lambda g, bm, rs: (bm[g], 0)
    fwd2 = lambda g, bm, rs: (0, g, 0)
    bwd2 = lambda g, bm, rs: (1, bm[g], 0)
    tokf = pl.BlockSpec((SCAN_BLOCK, c), fwd)
    tokb = pl.BlockSpec((SCAN_BLOCK, c), bwd)
    dirf = pl.BlockSpec((None, SCAN_BLOCK, c), fwd2)
    dirb = pl.BlockSpec((None, SCAN_BLOCK, c), bwd2)
    grid_spec = pltpu.PrefetchScalarGridSpec(
        num_scalar_prefetch=2,
        grid=(ns,),
        in_specs=[tokf, tokf, tokf, dirf, dirf, dirf, tokb, tokb, tokb, dirb, dirb, dirb],
        out_specs=[tokf, tokb],
        scratch_shapes=[pltpu.VMEM((N_PAIRS, 128, 128), F32), pltpu.VMEM((N_PAIRS, 128, 128), F32)],
    )
    tok = jax.ShapeDtypeStruct((ntok, c), F32)
    return pl.pallas_call(
        _scan_kernel,
        grid_spec=grid_spec,
        out_shape=[tok, tok],
        compiler_params=pltpu.CompilerParams(dimension_semantics=("arbitrary",), vmem_limit_bytes=VMEM_LIMIT),
        name="rwkv_scan",
    )(bmap_bwd, reset, r, kk, v, lw, kd, bd, r, kk, v, lw, kd, bd)


def _post_kernel(mrow_ref, x_ref, yf_ref, yb_ref, bonus_ref, g_ref, attn_ref, gt_ref, sh_ref, sc_ref,
                 g2_ref, lnw_ref, lnb_ref, ones_ref, wo_ref, x1_ref, h2_ref):
    del mrow_ref
    y = yf_ref[...] + yb_ref[...]
    ones_bd = ones_ref[...]
    y1, y2, _ = _split3(y)
    mu = (_dot(y1, ones_bd) + _dot(y2, ones_bd)) * (1.0 / HEAD_DIM)
    d = y - mu
    var = _dot((d * d).astype(BF16), ones_bd) * (1.0 / HEAD_DIM)
    yn = d * lax.rsqrt(var + LN_X_EPS) * lnw_ref[...] + lnb_ref[...]
    rw = ((yn + bonus_ref[...]) * g_ref[...]).astype(BF16)
    a = ATTN_WIDTH
    mix = _dot(attn_ref[...], wo_ref[0:a, :]) + _dot(rw, wo_ref[a:, :])
    x1 = x_ref[...] + gt_ref[0] * mix
    x1_ref[...] = x1
    h2_ref[...] = _modulated_rmsnorm(x1, g2_ref[...], sc_ref[0], sh_ref[0]).astype(BF16)


def _post(x, yf, yb, bonus, g, attn, mod3, mrow_tile, g_norm2, ln_x_w, ln_x_b, w_out_bf):
    ntok = x.shape[0]
    nt = ntok // ROW_TILE
    c = RWKV_WIDTH
    row = lambda i, mr: (i, 0)
    const = lambda i, mr: (0, 0)
    modspec = lambda j: pl.BlockSpec((1, 1, D_MODEL), lambda i, mr: (mr[i] * 6 + j, 0, 0))
    wide = pl.BlockSpec((ROW_TILE, D_MODEL), row)
    half = pl.BlockSpec((ROW_TILE, c), row)
    grid_spec = pltpu.PrefetchScalarGridSpec(
        num_scalar_prefetch=1,
        grid=(nt,),
        in_specs=[wide, half, half, half, half, half, modspec(2), modspec(3), modspec(4),
                  pl.BlockSpec((1, D_MODEL), const), pl.BlockSpec((1, c), const), pl.BlockSpec((1, c), const),
                  pl.BlockSpec((c, c), const), pl.BlockSpec((D_MODEL, D_MODEL), const)],
        out_specs=[wide, wide],
    )
    return pl.pallas_call(
        _post_kernel,
        grid_spec=grid_spec,
        out_shape=[jax.ShapeDtypeStruct((ntok, D_MODEL), F32), jax.ShapeDtypeStruct((ntok, D_MODEL), BF16)],
        compiler_params=pltpu.CompilerParams(dimension_semantics=("arbitrary",), vmem_limit_bytes=VMEM_LIMIT),
        name="post",
    )(mrow_tile, x, yf, yb, bonus, g, attn, mod3, mod3, mod3, g_norm2.reshape(1, -1),
      ln_x_w.reshape(1, -1), ln_x_b.reshape(1, -1), _head_ones(), w_out_bf)


def _ffn_kernel(mrow_ref, x1_ref, h2_ref, gt_ref, w1_ref, w2_ref, o_ref):
    del mrow_ref
    h = h2_ref[...]
    acc = jnp.zeros((h.shape[0], D_MODEL), F32)
    step = 1024
    for j in range(D_FF // step):
        a = jnp.maximum(_dot(h, w1_ref[:, j * step:(j + 1) * step]), 0.0)
        acc = acc + _dot((a * a).astype(BF16), w2_ref[j * step:(j + 1) * step, :])
    o_ref[...] = x1_ref[...] + gt_ref[0] * acc


def _ffn(x1, h2, mod3, mrow_tile, w1_bf, w2_bf):
    ntok = x1.shape[0]
    nt = ntok // ROW_TILE
    row = lambda i, mr: (i, 0)
    const = lambda i, mr: (0, 0)
    wide = pl.BlockSpec((ROW_TILE, D_MODEL), row)
    grid_spec = pltpu.PrefetchScalarGridSpec(
        num_scalar_prefetch=1,
        grid=(nt,),
        in_specs=[wide, wide, pl.BlockSpec((1, 1, D_MODEL), lambda i, mr: (mr[i] * 6 + 5, 0, 0)),
                  pl.BlockSpec((D_MODEL, D_FF), const), pl.BlockSpec((D_FF, D_MODEL), const)],
        out_specs=wide,
    )
    return pl.pallas_call(
        _ffn_kernel,
        grid_spec=grid_spec,
        out_shape=jax.ShapeDtypeStruct((ntok, D_MODEL), F32),
        compiler_params=pltpu.CompilerParams(dimension_semantics=("arbitrary",), vmem_limit_bytes=VMEM_LIMIT),
        name="ffn",
    )(mrow_tile, x1, h2, mod3, w1_bf, w2_bf)


def _sequence_tables(seq_blocks):
    mod_row, has_prev, has_next = [], [], []
    for s, n in enumerate(seq_blocks):
        for b in range(n):
            mod_row.append(s)
            has_prev.append(int(b > 0))
            has_next.append(int(b < n - 1))
    per = SEQ_BLOCK // SCAN_BLOCK
    bmap_bwd, reset = [], []
    start = 0
    for n in seq_blocks:
        ns = n * per
        for t in range(ns):
            bmap_bwd.append(start + ns - 1 - t)
            reset.append(int(t == 0))
        start += ns
    i32 = lambda a: jnp.asarray(np.asarray(a, np.int32))
    rep = lambda a, k: np.repeat(np.asarray(a, np.int32), k)
    row_tiles = SEQ_BLOCK // ROW_TILE
    prep_tiles = SEQ_BLOCK // PREP_TILE
    first = np.zeros(len(mod_row) * prep_tiles, np.int32)
    last = np.zeros(len(mod_row) * prep_tiles, np.int32)
    for b in range(len(mod_row)):
        if not has_prev[b]:
            first[b * prep_tiles] = 1
        if not has_next[b]:
            last[(b + 1) * prep_tiles - 1] = 1
    return dict(mrow_tile=i32(rep(mod_row, row_tiles)), has_prev=i32(has_prev), has_next=i32(has_next),
                bmap_bwd=i32(bmap_bwd), reset=i32(reset), first=i32(first), last=i32(last))


def _layer(x, c_all, tabs, w_ada, b_ada, g_norm1, g_norm2, w_in, q_norm_g, k_norm_g, attn_beta,
           mu_prev, mu_next, w0, w_up, a0, a_up, g_up, k_k, k_a, r_k, ln_x_w, ln_x_b, w_out, w_ff1, w_ff2):
    mod = _ada(c_all, w_ada, b_ada)
    mod3 = mod.reshape(-1, 1, D_MODEL)
    q, k, v, p_rw = _inproj(x, mod3, tabs["mrow_tile"], g_norm1, w_in.astype(BF16))
    attn = _attention(q, k, v, tabs["has_prev"], tabs["has_next"], q_norm_g, k_norm_g, attn_beta)
    r, kk, vv, g, bonus, lw, kd, bd = _prep(p_rw, tabs["first"], tabs["last"], mu_prev, mu_next, w0, w_up,
                                            a0, a_up, g_up, k_k, k_a, r_k)
    yf, yb = _scan(r, kk, vv, lw, kd, bd, tabs["bmap_bwd"], tabs["reset"])
    x1, h2 = _post(x, yf, yb, bonus, g, attn, mod3, tabs["mrow_tile"], g_norm2, ln_x_w, ln_x_b,
                   w_out.astype(BF16))
    return _ffn(x1, h2, mod3, tabs["mrow_tile"], w_ff1.astype(BF16), w_ff2.astype(BF16))


def kernel(x_prompt, x_sample, c_prompt, c_sample, w_ada, b_ada, g_norm1, g_norm2, w_in, q_norm_g, k_norm_g, attn_beta, mu_prev, mu_next, w0, w_up, a0, a_up, g_up, k_k, k_a, r_k, ln_x_w, ln_x_b, w_out, w_ff1, w_ff2):
    bp, sp, d = x_prompt.shape
    bs, ss, _ = x_sample.shape
    assert d == D_MODEL and sp % SEQ_BLOCK == 0 and ss % SEQ_BLOCK == 0
    seq_blocks = [sp // SEQ_BLOCK] * bp + [ss // SEQ_BLOCK] * bs
    tabs = _sequence_tables(seq_blocks)
    x = jnp.concatenate([x_prompt.reshape(bp * sp, d), x_sample.reshape(bs * ss, d)], axis=0)
    n_seq = bp + bs
    pad = (-n_seq) % 8
    c_all = jnp.concatenate([c_prompt, c_sample, jnp.zeros((pad, d), F32)], axis=0)
    y = x
    for i in range(w_ada.shape[0]):
        y = _layer(y, c_all, tabs, w_ada[i], b_ada[i], g_norm1[i], g_norm2[i], w_in[i], q_norm_g[i], k_norm_g[i],
                   attn_beta[i], mu_prev[i], mu_next[i], w0[i], w_up[i], a0[i], a_up[i], g_up[i], k_k[i], k_a[i],
                   r_k[i], ln_x_w[i], ln_x_b[i], w_out[i], w_ff1[i], w_ff2[i])
    return (y[:bp * sp].reshape(bp, sp, d), y[bp * sp:].reshape(bs, ss, d))
```

```python
import functools

import numpy as np
import jax
import jax.numpy as jnp
from jax import lax
from jax.experimental import pallas as pl
from jax.experimental.pallas import tpu as pltpu

F32 = jnp.float32
BF16 = jnp.bfloat16

D_MODEL = 1024
HEAD_DIM = 64
ATTN_WIDTH = 512
RWKV_WIDTH = 512
N_PAIRS = ATTN_WIDTH // 128
RWKV_IN = 1920
IN_WIDTH = 3 * ATTN_WIDTH + RWKV_IN
D_FF = 4096
DILATIONS = (1, 4, 16)
HALF = 64
NORM_EPS = 1e-6
LN_X_EPS = 64e-5
NEG_INF = -1e30
LOG2_E = 1.4426950408889634

SEQ_BLOCK = 2048
HALO = 1024
SCAN_BLOCK = 512
CHUNK = 64
ROW_TILE = 512
PREP_TILE = 256
BANDS_PER_STEP = 16
FF_STEP = 1024

VMEM_LIMIT = 56 * 1024 * 1024


def _dot(a, b):
    return jnp.dot(a, b, preferred_element_type=F32)


def _dot_nt(a, b):
    return lax.dot_general(a, b, (((1,), (1,)), ((), ())), preferred_element_type=F32)


def _dot_tn(a, b):
    return lax.dot_general(a, b, (((0,), (0,)), ((), ())), preferred_element_type=F32)


def _split3(x):
    hi = x.astype(BF16)
    r1 = x - hi.astype(F32)
    mid = r1.astype(BF16)
    lo = (r1 - mid.astype(F32)).astype(BF16)
    return hi, mid, lo


def _sigmoid(x):
    return 1.0 / (1.0 + jnp.exp(-x))


def _ada_kernel(c_ref, w_ref, b_ref, o_ref):
    c = c_ref[...]
    s = c * _sigmoid(c)
    s1, s2, _ = _split3(s)
    w = w_ref[...]
    w1, w2, _ = _split3(w)
    o_ref[...] = _dot(s1, w1) + (_dot(s1, w2) + _dot(s2, w1)) + b_ref[...]


def _ada(c_all, w_ada, b_ada):
    n = c_all.shape[0]
    nt = w_ada.shape[1] // D_MODEL
    return pl.pallas_call(
        _ada_kernel,
        grid=(nt,),
        in_specs=[pl.BlockSpec((n, D_MODEL), lambda j: (0, 0)),
                  pl.BlockSpec((D_MODEL, D_MODEL), lambda j: (0, j)),
                  pl.BlockSpec((1, D_MODEL), lambda j: (0, j))],
        out_specs=pl.BlockSpec((n, D_MODEL), lambda j: (0, j)),
        out_shape=jax.ShapeDtypeStruct((n, w_ada.shape[1]), F32),
        compiler_params=pltpu.CompilerParams(dimension_semantics=("arbitrary",), vmem_limit_bytes=VMEM_LIMIT),
        name="ada",
    )(c_all, w_ada, b_ada.reshape(1, -1))


def _modulated_rmsnorm(x, g, scale, shift):
    ms = jnp.mean(x * x, axis=-1, keepdims=True)
    return (x * lax.rsqrt(ms + NORM_EPS) * g) * (1.0 + scale) + shift


def _two_group_specs(n_prompt_tiles):
    def prompt(i, *_):
        return (jnp.minimum(i, n_prompt_tiles - 1), 0)

    def sample(i, *_):
        return (jnp.maximum(i - n_prompt_tiles, 0), 0)
    return pl.BlockSpec((ROW_TILE, D_MODEL), prompt), pl.BlockSpec((ROW_TILE, D_MODEL), sample)


def _head_rms_scale(z, ones_bd):
    ssq = _dot((z * z).astype(BF16), ones_bd)
    return lax.rsqrt(ssq * (1.0 / HEAD_DIM) + NORM_EPS)


def _inproj_kernel(n_prompt_tiles, mrow_ref, xp_ref, xs_ref, g_ref, sh_ref, sc_ref, w_ref, qg_ref, kg_ref,
                   ones_ref, q_ref, k_ref, v_ref, p_ref):
    del mrow_ref
    x = jnp.where(pl.program_id(0) < n_prompt_tiles, xp_ref[...], xs_ref[...])
    h = _modulated_rmsnorm(x, g_ref[...], sc_ref[0], sh_ref[0]).astype(BF16)
    a = ATTN_WIDTH
    ones_bd = ones_ref[...]
    q = _dot(h, w_ref[:, 0:a])
    q_ref[...] = q * _head_rms_scale(q, ones_bd) * (qg_ref[...] * (HEAD_DIM ** -0.5 * LOG2_E))
    k = _dot(h, w_ref[:, a:2 * a])
    k_ref[...] = k * _head_rms_scale(k, ones_bd) * kg_ref[...]
    v_ref[...] = _dot(h, w_ref[:, 2 * a:3 * a])
    p_ref[...] = _dot(h, w_ref[:, 3 * a:])


def _inproj(xp, xs, mod3, mrow_tile, g_norm1, w_in_bf, q_norm_g, k_norm_g):
    n_prompt_tiles = xp.shape[0] // ROW_TILE
    ntok = xp.shape[0] + xs.shape[0]
    nt = ntok // ROW_TILE
    row = lambda i, mr: (i, 0)
    const = lambda i, mr: (0, 0)
    xp_spec, xs_spec = _two_group_specs(n_prompt_tiles)
    n_heads = ATTN_WIDTH // HEAD_DIM
    per_head = lambda g: jnp.tile(g.reshape(1, HEAD_DIM), (1, n_heads))
    grid_spec = pltpu.PrefetchScalarGridSpec(
        num_scalar_prefetch=1,
        grid=(nt,),
        in_specs=[xp_spec, xs_spec,
                  pl.BlockSpec((1, D_MODEL), const),
                  pl.BlockSpec((1, 1, D_MODEL), lambda i, mr: (mr[i] * 6 + 0, 0, 0)),
                  pl.BlockSpec((1, 1, D_MODEL), lambda i, mr: (mr[i] * 6 + 1, 0, 0)),
                  pl.BlockSpec((D_MODEL, IN_WIDTH), const),
                  pl.BlockSpec((1, ATTN_WIDTH), const), pl.BlockSpec((1, ATTN_WIDTH), const),
                  pl.BlockSpec((ATTN_WIDTH, ATTN_WIDTH), const)],
        out_specs=[pl.BlockSpec((ROW_TILE, ATTN_WIDTH), row)] * 3 + [pl.BlockSpec((ROW_TILE, RWKV_IN), row)],
    )
    return pl.pallas_call(
        functools.partial(_inproj_kernel, n_prompt_tiles),
        grid_spec=grid_spec,
        out_shape=[jax.ShapeDtypeStruct((ntok, ATTN_WIDTH), F32)] * 3 + [jax.ShapeDtypeStruct((ntok, RWKV_IN), F32)],
        compiler_params=pltpu.CompilerParams(dimension_semantics=("arbitrary",), vmem_limit_bytes=VMEM_LIMIT),
        name="inproj",
    )(mrow_tile, xp, xs, g_norm1.reshape(1, -1), mod3, mod3, w_in_bf, per_head(q_norm_g), per_head(k_norm_g),
      _head_ones())


def _lane_is_head0(shape):
    return lax.broadcasted_iota(jnp.int32, shape, len(shape) - 1) < HEAD_DIM


def _attn_kernel(hasprev_ref, hasnext_ref,
                 q_ref, kp_ref, kc_ref, kx_ref, vp_ref, vc_ref, vx_ref, beta_ref, bias_ref,
                 o_ref,
                 qd, kd, vd, acc_o, acc_m):
    i = pl.program_id(0)
    has_prev = hasprev_ref[i]
    has_next = hasnext_ref[i]
    rows = 256

    m0_o = _lane_is_head0((HALF, 128))

    for di, dil in enumerate(DILATIONS):
        lq = SEQ_BLOCK // dil
        lk = lq + 2 * HALF
        nbq = lq // HALF
        pbase = HALO - HALF * dil

        if dil == 1:
            def cast_rows(src_ref, src_off, dst_ref, dst_off, nrows, chunk):
                def body(t, carry):
                    r0 = pl.multiple_of(t * chunk, chunk)
                    dst_ref[pl.ds(dst_off + r0, chunk), :] = src_ref[pl.ds(src_off + r0, chunk), :].astype(BF16)
                    return carry
                lax.fori_loop(0, nrows // chunk, body, 0)
            cast_rows(q_ref, 0, qd, 0, lq, rows)
            for prev_ref, cur_ref, next_ref, dst in ((kp_ref, kc_ref, kx_ref, kd), (vp_ref, vc_ref, vx_ref, vd)):
                cast_rows(prev_ref, pbase, dst, 0, HALF, HALF)
                cast_rows(cur_ref, 0, dst, HALF, lq, rows)
                cast_rows(next_ref, 0, dst, HALF + lq, HALF, HALF)
        else:
            def regroup(r, carry, dil=dil, lq=lq, lk=lk, pbase=pbase):
                qd[pl.ds(pl.multiple_of(r * lq, HALF), lq), :] = q_ref[pl.ds(r, lq, stride=dil), :].astype(BF16)
                ko = pl.multiple_of(r * lk, HALF)
                for prev_ref, cur_ref, next_ref, dst in ((kp_ref, kc_ref, kx_ref, kd), (vp_ref, vc_ref, vx_ref, vd)):
                    dst[pl.ds(ko, HALF), :] = prev_ref[pl.ds(pbase + r, HALF, stride=dil), :].astype(BF16)
                    dst[pl.ds(ko + HALF, lq), :] = cur_ref[pl.ds(r, lq, stride=dil), :].astype(BF16)
                    dst[pl.ds(ko + HALF + lq, HALF), :] = next_ref[pl.ds(r, HALF, stride=dil), :].astype(BF16)
                return carry
            lax.fori_loop(0, dil, regroup, 0)

        shift = nbq.bit_length() - 1

        def bands(it, carry, di=di, dil=dil, nbq=nbq, shift=shift):
            work = []
            for u in range(BANDS_PER_STEP):
                idx = it * BANDS_PER_STEP + u
                r = idx >> shift
                b = idx & (nbq - 1)
                qo = pl.multiple_of(idx * HALF, HALF)
                ko = pl.multiple_of(idx * HALF + r * (2 * HALF), HALF)
                variant = jnp.where(jnp.logical_and(b == 0, has_prev == 0), 1,
                                    jnp.where(jnp.logical_and(b == nbq - 1, has_next == 0), 2, 0))
                work.append((r + (dil * HALF) * b, di * 3 + variant, qd[pl.ds(qo, HALF), :],
                             kd[pl.ds(ko, 3 * HALF), :], vd[pl.ds(ko, 3 * HALF), :]))
            s, m, l, o = [], [], [], []
            for w in work:
                si = _dot_nt(_stack_heads(w[2]), w[3]) + bias_ref[w[1]]
                s.append(si)
                m.append(jnp.max(si, axis=-1, keepdims=True))
            for w, si, mi in zip(work, s, m):
                pi = jnp.exp2(si - mi)
                l.append(jnp.sum(pi, axis=-1, keepdims=True))
                o.append(_dot(pi.astype(BF16), w[4]))
            for w, oi, mi, li in zip(work, o, m, l):
                if dil == 1:
                    dst = pl.ds(pl.multiple_of(w[0], HALF), HALF)
                else:
                    dst = pl.ds(w[0], HALF, stride=dil)
                on = oi * (1.0 / li)
                lse = mi + jnp.log2(li)
                acc_o[di][dst, :] = jnp.where(m0_o, on[0:HALF], on[HALF:2 * HALF])
                acc_m[di][dst, :] = jnp.where(m0_o, lse[0:HALF], lse[HALF:2 * HALF])
            return carry

        lax.fori_loop(0, SEQ_BLOCK // HALF // BANDS_PER_STEP, bands, 0)

    beta = beta_ref[...]

    def merge(t, carry):
        r0 = pl.multiple_of(t * rows, rows)
        sl = pl.ds(r0, rows)
        ms = [acc_m[d][sl, :] for d in range(3)]
        mx = jnp.maximum(jnp.maximum(ms[0], ms[1]), ms[2])
        num = jnp.zeros((rows, 128), F32)
        den = jnp.zeros((rows, 128), F32)
        for d in range(3):
            e = jnp.exp2(ms[d] - mx)
            num = num + e * acc_o[d][sl, :]
            den = den + e
        o_ref[sl, :] = ((num / den) * beta).astype(o_ref.dtype)
        return carry

    lax.fori_loop(0, SEQ_BLOCK // rows, merge, 0)


def _attn_bias_table():
    qi = np.arange(HALF)[:, None]
    kj = np.arange(3 * HALF)[None, :]
    rel = np.abs(kj - HALF - qi).astype(np.float32)
    n_heads = ATTN_WIDTH // HEAD_DIM
    slopes = 2.0 ** (-8.0 * (np.arange(n_heads, dtype=np.float32) + 1.0) / n_heads)
    keep = [rel <= HALF, (rel <= HALF) & (kj >= HALF), (rel <= HALF) & (kj < 2 * HALF)]
    tab = np.empty((N_PAIRS, 3, 3, 2, HALF, 3 * HALF), np.float32)
    for h in range(n_heads):
        for di, dil in enumerate(DILATIONS):
            for var in range(3):
                tab[h // 2, di, var, h % 2] = np.where(keep[var], -(slopes[h] * (dil * rel)) * LOG2_E, NEG_INF)
    return jnp.asarray(tab.reshape(N_PAIRS * 9, 2 * HALF, 3 * HALF))


def _attention(q, k, v, has_prev, has_next, attn_beta):
    ntok = q.shape[0]
    nb = ntok // SEQ_BLOCK
    hb = SEQ_BLOCK // HALO
    cur = lambda i, j, hp, hn: (i, j)
    prev = lambda i, j, hp, hn: (jnp.maximum(i * hb - 1, 0), j)
    nxt = lambda i, j, hp, hn: (jnp.minimum((i + 1) * hb, nb * hb - 1), j)
    blk = pl.BlockSpec((SEQ_BLOCK, 128), cur)
    halo_p = pl.BlockSpec((HALO, 128), prev)
    halo_n = pl.BlockSpec((HALO, 128), nxt)
    win = SEQ_BLOCK + 2 * HALO
    grid_spec = pltpu.PrefetchScalarGridSpec(
        num_scalar_prefetch=2,
        grid=(nb, N_PAIRS),
        in_specs=[blk, halo_p, blk, halo_n, halo_p, blk, halo_n,
                  pl.BlockSpec((1, 128), lambda i, j, hp, hn: (0, j)),
                  pl.BlockSpec((9, 2 * HALF, 3 * HALF), lambda i, j, hp, hn: (j, 0, 0))],
        out_specs=blk,
        scratch_shapes=[pltpu.VMEM((SEQ_BLOCK, 128), BF16), pltpu.VMEM((win, 128), BF16), pltpu.VMEM((win, 128), BF16),
                        [pltpu.VMEM((SEQ_BLOCK, 128), F32)] * 3,
                        [pltpu.VMEM((SEQ_BLOCK, 128), F32)] * 3],
    )
    return pl.pallas_call(
        _attn_kernel,
        grid_spec=grid_spec,
        out_shape=jax.ShapeDtypeStruct((ntok, ATTN_WIDTH), BF16),
        compiler_params=pltpu.CompilerParams(dimension_semantics=("arbitrary", "arbitrary"),
                                             vmem_limit_bytes=VMEM_LIMIT),
        name="attn",
    )(has_prev, has_next, q, k, k, k, v, v, v, attn_beta.reshape(1, -1), _attn_bias_table())


def _prep_kernel(first_ref, last_ref,
                 p_ref, hp_ref, hn_ref, mup_ref, mun_ref, w0_ref, wup_ref, a0_ref, aup_ref, gup_ref,
                 kk_ref, ka_ref, rk_ref, ones_ref,
                 r_out, kk_out, v_out, g_out, bonus_out, lw_out, kd_out, bd_out):
    i = pl.program_id(0)
    tm = p_ref.shape[0]
    p = p_ref[...]
    row = lax.broadcasted_iota(jnp.int32, (tm, 1), 0)
    keep_prev = jnp.where(first_ref[i] == 0, 1.0, 0.0)
    keep_next = jnp.where(last_ref[i] == 0, 1.0, 0.0)
    prev = jnp.where(row == 0, hp_ref[7:8, :] * keep_prev, pltpu.roll(p, 1, 0))
    nxt = jnp.where(row == tm - 1, hn_ref[0:1, :] * keep_next, pltpu.roll(p, tm - 1, 0))
    ps = p + mup_ref[...] * (prev - p) + mun_ref[...] * (nxt - p)

    c = RWKV_WIDTH
    r = ps[:, 0:c]
    k = ps[:, c:2 * c]
    v = ps[:, 2 * c:3 * c]
    wd = ps[:, 3 * c:3 * c + 128]
    ad = ps[:, 3 * c + 128:3 * c + 256]
    gd = ps[:, 3 * c + 256:3 * c + 384]

    w_raw = w0_ref[...] + _dot(jnp.tanh(wd).astype(BF16), wup_ref[...])
    lw = (-float(np.exp(-0.5))) * _sigmoid(w_raw)
    a = _sigmoid(a0_ref[...] + _dot(ad.astype(BF16), aup_ref[...]))
    g = _dot(_sigmoid(gd).astype(BF16), gup_ref[...])

    ones_bd = ones_ref[...]
    kkv = k * kk_ref[...]
    ssq = _dot((kkv * kkv).astype(BF16), ones_bd)
    kkv = kkv * lax.rsqrt(jnp.maximum(ssq, 1e-24))
    ka = ka_ref[...]
    kd0 = k * (1.0 + (a[:, 0:c] - 1.0) * ka)
    kd1 = k * (1.0 + (a[:, c:2 * c] - 1.0) * ka)
    hsum = _dot((r * (kd0 + kd1) * rk_ref[...]).astype(BF16), ones_bd)

    r_out[...] = r.astype(BF16)
    kk_out[...] = kkv.astype(BF16)
    v_out[...] = v.astype(BF16)
    g_out[...] = g
    bonus_out[...] = hsum * v
    lw_out[0] = lw[:, 0:c]
    lw_out[1] = lw[:, c:2 * c]
    kd_out[0] = kd0.astype(BF16)
    kd_out[1] = kd1.astype(BF16)
    bd_out[0] = (kkv * a[:, 0:c]).astype(BF16)
    bd_out[1] = (kkv * a[:, c:2 * c]).astype(BF16)


def _block_diag2(m):
    z = jnp.zeros_like(m[0])
    return jnp.concatenate([jnp.concatenate([m[0], z], axis=1), jnp.concatenate([z, m[1]], axis=1)], axis=0)


def _head_ones():
    idx = np.arange(RWKV_WIDTH) // HEAD_DIM
    return jnp.asarray((idx[:, None] == idx[None, :]).astype(np.float32), dtype=BF16)


def _prep(p_rw, first_tile, last_tile, mu_prev, mu_next, w0, w_up, a0, a_up, g_up, k_k, k_a, r_k):
    ntok = p_rw.shape[0]
    tm = PREP_TILE
    nt = ntok // tm
    c = RWKV_WIDTH
    row = lambda i, f, l: (i, 0)
    const = lambda i, f, l: (0, 0)
    grid_spec = pltpu.PrefetchScalarGridSpec(
        num_scalar_prefetch=2,
        grid=(nt,),
        in_specs=[pl.BlockSpec((tm, RWKV_IN), row),
                  pl.BlockSpec((8, RWKV_IN), lambda i, f, l: (jnp.maximum(i * (tm // 8) - 1, 0), 0)),
                  pl.BlockSpec((8, RWKV_IN), lambda i, f, l: (jnp.minimum((i + 1) * (tm // 8), ntok // 8 - 1), 0)),
                  pl.BlockSpec((1, RWKV_IN), const), pl.BlockSpec((1, RWKV_IN), const),
                  pl.BlockSpec((1, 2 * c), const), pl.BlockSpec((128, 2 * c), const),
                  pl.BlockSpec((1, 2 * c), const), pl.BlockSpec((128, 2 * c), const),
                  pl.BlockSpec((128, c), const),
                  pl.BlockSpec((1, c), const), pl.BlockSpec((1, c), const), pl.BlockSpec((1, c), const),
                  pl.BlockSpec((c, c), const)],
        out_specs=[pl.BlockSpec((tm, c), row)] * 5 + [pl.BlockSpec((2, tm, c), lambda i, f, l: (0, i, 0))] * 3,
    )
    tok = lambda dt: jax.ShapeDtypeStruct((ntok, c), dt)
    tok2 = lambda dt: jax.ShapeDtypeStruct((2, ntok, c), dt)
    return pl.pallas_call(
        _prep_kernel,
        grid_spec=grid_spec,
        out_shape=[tok(BF16), tok(BF16), tok(BF16), tok(F32), tok(F32), tok2(F32), tok2(BF16), tok2(BF16)],
        compiler_params=pltpu.CompilerParams(dimension_semantics=("arbitrary",), vmem_limit_bytes=VMEM_LIMIT),
        name="rwkv_prep",
    )(first_tile, last_tile, p_rw, p_rw, p_rw, mu_prev.reshape(1, -1), mu_next.reshape(1, -1),
      w0.reshape(1, -1), _block_diag2(w_up).astype(BF16), a0.reshape(1, -1), _block_diag2(a_up).astype(BF16),
      g_up.astype(BF16), k_k.reshape(1, -1), k_a.reshape(1, -1), r_k.reshape(1, -1), _head_ones())


def _stack_heads(x):
    m0 = _lane_is_head0(x.shape)
    z = jnp.zeros_like(x)
    return jnp.concatenate([jnp.where(m0, x, z), jnp.where(m0, z, x)], axis=0)


def _stack_heads_bf16(x):
    return _stack_heads(x.astype(BF16))


def _scan_masks(rev):
    c = CHUNK
    rowi = lax.broadcasted_iota(jnp.int32, (c, 128), 0)
    colj = lax.broadcasted_iota(jnp.int32, (c, 128), 1) & (c - 1)
    ti = lax.broadcasted_iota(jnp.int32, (c, c), 0)
    tj = lax.broadcasted_iota(jnp.int32, (c, c), 1)
    if rev:
        strict, incl, tri = colj > rowi, colj >= rowi, tj >= ti
    else:
        strict, incl, tri = colj < rowi, colj <= rowi, tj <= ti
    return dict(strict=strict, incl=incl, tri=tri.astype(BF16), eye2=(colj == rowi).astype(F32))


def _scan_chunks(ins, zs, masks):
    c = CHUNK
    n = len(ins)
    idx = range(n)
    lw = [a[0] for a in ins]
    v = [a[3] for a in ins]

    cl = []
    for i in idx:
        hi, mid, lo = _split3(lw[i])
        cl3 = _dot(masks[i]["tri"], jnp.concatenate([hi, mid, lo], axis=1))
        cl.append(cl3[:, 0:128] + cl3[:, 128:256] + cl3[:, 256:384])
    g_tot = [jnp.exp(cl[i][0:1, :] if masks[i]["rev"] else cl[i][c - 1:c, :]) for i in idx]
    g_inv = [jnp.exp(-cl[i]) for i in idx]
    rt = [ins[i][1] * jnp.exp(cl[i]) for i in idx]
    bt = [ins[i][2] * jnp.exp(cl[i] - lw[i]) for i in idx]
    at = [-(ins[i][5] * g_inv[i]) for i in idx]
    kt = [ins[i][4] * g_inv[i] for i in idx]
    br = [jnp.concatenate([bt[i], rt[i]], axis=0).astype(BF16) for i in idx]
    lm = [_dot_nt(br[i], jnp.concatenate([_stack_heads_bf16(at[i]), _stack_heads_bf16(kt[i])], axis=0))
          for i in idx]
    la = [jnp.where(masks[i]["strict"], lm[i][0:c, 0:128], 0.0) for i in idx]
    lk = [jnp.where(masks[i]["strict"], lm[i][0:c, 128:256], 0.0).astype(BF16) for i in idx]
    mam = [jnp.concatenate([jnp.where(masks[i]["incl"], lm[i][c:2 * c, 0:128], 0.0),
                            jnp.where(masks[i]["incl"], lm[i][c:2 * c, 128:256], 0.0)], axis=1).astype(BF16)
           for i in idx]

    sv = [_stack_heads_bf16(v[i]) for i in idx]
    brz = [_dot_nt(br[i], zs[i].astype(BF16)) for i in idx]
    x = [brz[i][0:c] + _dot(lk[i], sv[i]) for i in idx]

    t = [masks[i]["eye2"] + la[i] for i in idx]
    pw = [_dot(la[i].astype(BF16), _stack_heads_bf16(la[i])) for i in idx]
    for step in range(5):
        if step < 4:
            out = [_dot(pw[i].astype(BF16),
                        jnp.concatenate([_stack_heads_bf16(pw[i]), _stack_heads_bf16(t[i])], axis=1))
                   for i in idx]
            t = [t[i] + out[i][:, 128:256] for i in idx]
            pw = [out[i][:, 0:128] for i in idx]
        else:
            t = [t[i] + _dot(pw[i].astype(BF16), _stack_heads_bf16(t[i])) for i in idx]

    u = [_dot(t[i].astype(BF16), _stack_heads_bf16(x[i])) for i in idx]
    y = [brz[i][c:2 * c] + _dot(mam[i], jnp.concatenate([_stack_heads_bf16(u[i]), sv[i]], axis=0)) for i in idx]
    zr = lax.broadcasted_iota(jnp.int32, (128, 128), 0) < HEAD_DIM
    zc = lax.broadcasted_iota(jnp.int32, (128, 128), 1) < HEAD_DIM
    same_head = zr == zc
    z_new = []
    for i in idx:
        uv = jnp.concatenate([u[i].astype(BF16), v[i].astype(BF16)], axis=0)
        akg = jnp.concatenate([at[i] * g_tot[i], kt[i] * g_tot[i]], axis=0).astype(BF16)
        z_new.append(zs[i] * g_tot[i] + jnp.where(same_head, _dot_tn(uv, akg), 0.0))
    return y, z_new


def _scan_kernel(bmap_ref, reset_ref,
                 rf, kkf, vf, lwf, kdf, bdf, rb, kkb, vb, lwb, kdb, bdb,
                 yf_ref, yb_ref, zf, zb):
    del bmap_ref
    g = pl.program_id(0)

    @pl.when(reset_ref[g] == 1)
    def _():
        zf[...] = jnp.zeros_like(zf)
        zb[...] = jnp.zeros_like(zb)

    nchunk = SCAN_BLOCK // CHUNK
    mask_f = dict(_scan_masks(False), rev=False)
    mask_b = dict(_scan_masks(True), rev=True)
    dirs = ((mask_f, (lwf, rf, kkf, vf, kdf, bdf), yf_ref, zf),
            (mask_b, (lwb, rb, kkb, vb, kdb, bdb), yb_ref, zb))

    def body(ci, carry):
        ins, zs, masks, outs = [], [], [], []
        for mask, refs, y_ref, z_ref in dirs:
            cc = (nchunk - 1 - ci) if mask["rev"] else ci
            rows = pl.ds(pl.multiple_of(cc * CHUNK, CHUNK), CHUNK)
            for pr in range(N_PAIRS):
                lanes = slice(128 * pr, 128 * (pr + 1))
                ins.append([ref[rows, lanes] for ref in refs])
                zs.append(z_ref[pr])
                masks.append(mask)
                outs.append((y_ref, z_ref, rows, lanes, pr))
        ys, z_new = _scan_chunks(ins, zs, masks)
        for (y_ref, z_ref, rows, lanes, pr), y, z in zip(outs, ys, z_new):
            y_ref[rows, lanes] = y
            z_ref[pr] = z
        return carry

    lax.fori_loop(0, nchunk, body, 0)


def _scan(r, kk, v, lw, kd, bd, bmap_bwd, reset):
    ntok = r.shape[0]
    c = RWKV_WIDTH
    ns = ntok // SCAN_BLOCK
    fwd = lambda g, bm, rs: (g, 0)
    bwd = lambda g, bm, rs: (bm[g], 0)
    fwd2 = lambda g, bm, rs: (0, g, 0)
    bwd2 = lambda g, bm, rs: (1, bm[g], 0)
    tokf = pl.BlockSpec((SCAN_BLOCK, c), fwd)
    tokb = pl.BlockSpec((SCAN_BLOCK, c), bwd)
    dirf = pl.BlockSpec((None, SCAN_BLOCK, c), fwd2)
    dirb = pl.BlockSpec((None, SCAN_BLOCK, c), bwd2)
    grid_spec = pltpu.PrefetchScalarGridSpec(
        num_scalar_prefetch=2,
        grid=(ns,),
        in_specs=[tokf, tokf, tokf, dirf, dirf, dirf, tokb, tokb, tokb, dirb, dirb, dirb],
        out_specs=[tokf, tokb],
        scratch_shapes=[pltpu.VMEM((N_PAIRS, 128, 128), F32), pltpu.VMEM((N_PAIRS, 128, 128), F32)],
    )
    tok = jax.ShapeDtypeStruct((ntok, c), F32)
    return pl.pallas_call(
        _scan_kernel,
        grid_spec=grid_spec,
        out_shape=[tok, tok],
        compiler_params=pltpu.CompilerParams(dimension_semantics=("arbitrary",), vmem_limit_bytes=VMEM_LIMIT),
        name="rwkv_scan",
    )(bmap_bwd, reset, r, kk, v, lw, kd, bd, r, kk, v, lw, kd, bd)


def _post_ffn_kernel(n_prompt_tiles, mrow_ref, xp_ref, xs_ref, yf_ref, yb_ref, bonus_ref, g_ref, attn_ref,
                     gt1_ref, sh2_ref, sc2_ref, gt2_ref, g2_ref, lnw_ref, lnb_ref, ones_ref, wo_ref, w1_ref, w2_ref,
                     op_ref, os_ref):
    del mrow_ref
    is_prompt = pl.program_id(0) < n_prompt_tiles
    x = jnp.where(is_prompt, xp_ref[...], xs_ref[...])

    y = yf_ref[...] + yb_ref[...]
    ones_bd = ones_ref[...]
    y1, y2, _ = _split3(y)
    mu = (_dot(y1, ones_bd) + _dot(y2, ones_bd)) * (1.0 / HEAD_DIM)
    d = y - mu
    var = _dot((d * d).astype(BF16), ones_bd) * (1.0 / HEAD_DIM)
    yn = d * lax.rsqrt(var + LN_X_EPS) * lnw_ref[...] + lnb_ref[...]
    rw = ((yn + bonus_ref[...]) * g_ref[...]).astype(BF16)

    a = ATTN_WIDTH
    mix = _dot(attn_ref[...], wo_ref[0:a, :]) + _dot(rw, wo_ref[a:, :])
    x1 = x + gt1_ref[0] * mix
    h = _modulated_rmsnorm(x1, g2_ref[...], sc2_ref[0], sh2_ref[0]).astype(BF16)

    acc = jnp.zeros((h.shape[0], D_MODEL), F32)
    for j in range(D_FF // FF_STEP):
        cols = slice(j * FF_STEP, (j + 1) * FF_STEP)
        act = jnp.maximum(_dot(h, w1_ref[:, cols]), 0.0)
        acc = acc + _dot((act * act).astype(BF16), w2_ref[cols, :])
    out = x1 + gt2_ref[0] * acc

    @pl.when(is_prompt)
    def _():
        op_ref[...] = out

    @pl.when(jnp.logical_not(is_prompt))
    def _():
        os_ref[...] = out


def _post_ffn(xp, xs, yf, yb, bonus, g, attn, mod3, mrow_tile, g_norm2, ln_x_w, ln_x_b, w_out_bf, w1_bf, w2_bf):
    n_prompt_tiles = xp.shape[0] // ROW_TILE
    nt = (xp.shape[0] + xs.shape[0]) // ROW_TILE
    c = RWKV_WIDTH
    row = lambda i, mr: (i, 0)
    const = lambda i, mr: (0, 0)
    resident = lambda shape: pl.BlockSpec(shape, const, pipeline_mode=pl.Buffered(1))
    modspec = lambda j: pl.BlockSpec((1, 1, D_MODEL), lambda i, mr: (mr[i] * 6 + j, 0, 0))
    xp_spec, xs_spec = _two_group_specs(n_prompt_tiles)
    half = pl.BlockSpec((ROW_TILE, c), row)
    grid_spec = pltpu.PrefetchScalarGridSpec(
        num_scalar_prefetch=1,
        grid=(nt,),
        in_specs=[xp_spec, xs_spec, half, half, half, half, half,
                  modspec(2), modspec(3), modspec(4), modspec(5),
                  pl.BlockSpec((1, D_MODEL), const), pl.BlockSpec((1, c), const), pl.BlockSpec((1, c), const),
                  resident((c, c)), resident((D_MODEL, D_MODEL)), resident((D_MODEL, D_FF)),
                  resident((D_FF, D_MODEL))],
        out_specs=[xp_spec, xs_spec],
    )
    return pl.pallas_call(
        functools.partial(_post_ffn_kernel, n_prompt_tiles),
        grid_spec=grid_spec,
        out_shape=[jax.ShapeDtypeStruct(xp.shape, F32), jax.ShapeDtypeStruct(xs.shape, F32)],
        compiler_params=pltpu.CompilerParams(dimension_semantics=("arbitrary",), vmem_limit_bytes=VMEM_LIMIT),
        name="post_ffn",
    )(mrow_tile, xp, xs, yf, yb, bonus, g, attn, mod3, mod3, mod3, mod3, g_norm2.reshape(1, -1),
      ln_x_w.reshape(1, -1), ln_x_b.reshape(1, -1), _head_ones(), w_out_bf, w1_bf, w2_bf)


def _sequence_tables(seq_blocks):
    mod_row, has_prev, has_next = [], [], []
    for s, n in enumerate(seq_blocks):
        for b in range(n):
            mod_row.append(s)
            has_prev.append(int(b > 0))
            has_next.append(int(b < n - 1))
    per = SEQ_BLOCK // SCAN_BLOCK
    bmap_bwd, reset = [], []
    start = 0
    for n in seq_blocks:
        ns = n * per
        for t in range(ns):
            bmap_bwd.append(start + ns - 1 - t)
            reset.append(int(t == 0))
        start += ns
    i32 = lambda a: jnp.asarray(np.asarray(a, np.int32))
    rep = lambda a, k: np.repeat(np.asarray(a, np.int32), k)
    row_tiles = SEQ_BLOCK // ROW_TILE
    prep_tiles = SEQ_BLOCK // PREP_TILE
    first = np.zeros(len(mod_row) * prep_tiles, np.int32)
    last = np.zeros(len(mod_row) * prep_tiles, np.int32)
    for b in range(len(mod_row)):
        if not has_prev[b]:
            first[b * prep_tiles] = 1
        if not has_next[b]:
            last[(b + 1) * prep_tiles - 1] = 1
    return dict(mrow_tile=i32(rep(mod_row, row_tiles)), has_prev=i32(has_prev), has_next=i32(has_next),
                bmap_bwd=i32(bmap_bwd), reset=i32(reset), first=i32(first), last=i32(last))


def _layer(xp, xs, c_all, tabs, w_ada, b_ada, g_norm1, g_norm2, w_in, q_norm_g, k_norm_g, attn_beta,
           mu_prev, mu_next, w0, w_up, a0, a_up, g_up, k_k, k_a, r_k, ln_x_w, ln_x_b, w_out, w_ff1, w_ff2):
    mod = _ada(c_all, w_ada, b_ada)
    mod3 = mod.reshape(-1, 1, D_MODEL)
    q, k, v, p_rw = _inproj(xp, xs, mod3, tabs["mrow_tile"], g_norm1, w_in.astype(BF16), q_norm_g, k_norm_g)
    attn = _attention(q, k, v, tabs["has_prev"], tabs["has_next"], attn_beta)
    r, kk, vv, g, bonus, lw, kd, bd = _prep(p_rw, tabs["first"], tabs["last"], mu_prev, mu_next, w0, w_up,
                                            a0, a_up, g_up, k_k, k_a, r_k)
    yf, yb = _scan(r, kk, vv, lw, kd, bd, tabs["bmap_bwd"], tabs["reset"])
    return _post_ffn(xp, xs, yf, yb, bonus, g, attn, mod3, tabs["mrow_tile"], g_norm2, ln_x_w, ln_x_b,
                     w_out.astype(BF16), w_ff1.astype(BF16), w_ff2.astype(BF16))


def kernel(x_prompt, x_sample, c_prompt, c_sample, w_ada, b_ada, g_norm1, g_norm2, w_in, q_norm_g, k_norm_g, attn_beta, mu_prev, mu_next, w0, w_up, a0, a_up, g_up, k_k, k_a, r_k, ln_x_w, ln_x_b, w_out, w_ff1, w_ff2):
    bp, sp, d = x_prompt.shape
    bs, ss, _ = x_sample.shape
    assert d == D_MODEL and sp % SEQ_BLOCK == 0 and ss % SEQ_BLOCK == 0
    seq_blocks = [sp // SEQ_BLOCK] * bp + [ss // SEQ_BLOCK] * bs
    tabs = _sequence_tables(seq_blocks)
    n_seq = bp + bs
    pad = (-n_seq) % 8
    c_all = jnp.concatenate([c_prompt, c_sample, jnp.zeros((pad, d), F32)], axis=0)
    yp = x_prompt.reshape(bp * sp, d)
    ys = x_sample.reshape(bs * ss, d)
    for i in range(w_ada.shape[0]):
        yp, ys = _layer(yp, ys, c_all, tabs, w_ada[i], b_ada[i], g_norm1[i], g_norm2[i], w_in[i], q_norm_g[i],
                        k_norm_g[i], attn_beta[i], mu_prev[i], mu_next[i], w0[i], w_up[i], a0[i], a_up[i], g_up[i],
                        k_k[i], k_a[i], r_k[i], ln_x_w[i], ln_x_b[i], w_out[i], w_ff1[i], w_ff2[i])
    return (yp.reshape(bp, sp, d), ys.reshape(bs, ss, d))
```

```python
import functools

import numpy as np
import jax
import jax.numpy as jnp
from jax import lax
from jax.experimental import pallas as pl
from jax.experimental.pallas import tpu as pltpu

F32 = jnp.float32
BF16 = jnp.bfloat16

D_MODEL = 1024
HEAD_DIM = 64
ATTN_WIDTH = 512
RWKV_WIDTH = 512
N_PAIRS = ATTN_WIDTH // 128
RWKV_IN = 1920
IN_WIDTH = 3 * ATTN_WIDTH + RWKV_IN
D_FF = 4096
DILATIONS = (1, 4, 16)
HALF = 64
NORM_EPS = 1e-6
LN_X_EPS = 64e-5
NEG_INF = -1e30
LOG2_E = 1.4426950408889634

SEQ_BLOCK = 2048
HALO = 1024
SCAN_BLOCK = 512
CHUNK = 64
CHUNKS_PER_GROUP = 2
ROW_TILE = 512
PREP_TILE = 256
BANDS_PER_STEP = 16
FF_STEP = 1024

VMEM_LIMIT = 56 * 1024 * 1024


def _dot(a, b):
    return jnp.dot(a, b, preferred_element_type=F32)


def _dot_nt(a, b):
    return lax.dot_general(a, b, (((1,), (1,)), ((), ())), preferred_element_type=F32)


def _dot_tn(a, b):
    return lax.dot_general(a, b, (((0,), (0,)), ((), ())), preferred_element_type=F32)


def _split3(x):
    hi = x.astype(BF16)
    r1 = x - hi.astype(F32)
    mid = r1.astype(BF16)
    lo = (r1 - mid.astype(F32)).astype(BF16)
    return hi, mid, lo


def _sigmoid(x):
    return 1.0 / (1.0 + jnp.exp(-x))


def _ada_kernel(c_ref, w_ref, b_ref, o_ref):
    c = c_ref[...]
    s = c * _sigmoid(c)
    s1, s2, _ = _split3(s)
    w = w_ref[...]
    w1, w2, _ = _split3(w)
    o_ref[...] = _dot(s1, w1) + (_dot(s1, w2) + _dot(s2, w1)) + b_ref[...]


def _ada(c_all, w_ada, b_ada):
    n = c_all.shape[0]
    nt = w_ada.shape[1] // D_MODEL
    return pl.pallas_call(
        _ada_kernel,
        grid=(nt,),
        in_specs=[pl.BlockSpec((n, D_MODEL), lambda j: (0, 0)),
                  pl.BlockSpec((D_MODEL, D_MODEL), lambda j: (0, j)),
                  pl.BlockSpec((1, D_MODEL), lambda j: (0, j))],
        out_specs=pl.BlockSpec((n, D_MODEL), lambda j: (0, j)),
        out_shape=jax.ShapeDtypeStruct((n, w_ada.shape[1]), F32),
        compiler_params=pltpu.CompilerParams(dimension_semantics=("arbitrary",), vmem_limit_bytes=VMEM_LIMIT),
        name="ada",
    )(c_all, w_ada, b_ada.reshape(1, -1))


def _modulated_rmsnorm(x, g, scale, shift):
    ms = jnp.mean(x * x, axis=-1, keepdims=True)
    return (x * lax.rsqrt(ms + NORM_EPS) * g) * (1.0 + scale) + shift


def _two_group_specs(n_prompt_tiles):
    def prompt(i, *_):
        return (jnp.minimum(i, n_prompt_tiles - 1), 0)

    def sample(i, *_):
        return (jnp.maximum(i - n_prompt_tiles, 0), 0)
    return pl.BlockSpec((ROW_TILE, D_MODEL), prompt), pl.BlockSpec((ROW_TILE, D_MODEL), sample)


def _head_rms_scale(z, ones_bd):
    ssq = _dot((z * z).astype(BF16), ones_bd)
    return lax.rsqrt(ssq * (1.0 / HEAD_DIM) + NORM_EPS)


def _inproj_kernel(n_prompt_tiles, mrow_ref, xp_ref, xs_ref, g_ref, sh_ref, sc_ref, w_ref, qg_ref, kg_ref,
                   ones_ref, q_ref, k_ref, v_ref, p_ref):
    del mrow_ref
    x = jnp.where(pl.program_id(0) < n_prompt_tiles, xp_ref[...], xs_ref[...])
    h = _modulated_rmsnorm(x, g_ref[...], sc_ref[0], sh_ref[0]).astype(BF16)
    a = ATTN_WIDTH
    ones_bd = ones_ref[...]
    q = _dot(h, w_ref[:, 0:a])
    q_ref[...] = q * _head_rms_scale(q, ones_bd) * (qg_ref[...] * (HEAD_DIM ** -0.5 * LOG2_E))
    k = _dot(h, w_ref[:, a:2 * a])
    k_ref[...] = k * _head_rms_scale(k, ones_bd) * kg_ref[...]
    v_ref[...] = _dot(h, w_ref[:, 2 * a:3 * a])
    p_ref[...] = _dot(h, w_ref[:, 3 * a:])


def _inproj(xp, xs, mod3, mrow_tile, g_norm1, w_in_bf, q_norm_g, k_norm_g):
    n_prompt_tiles = xp.shape[0] // ROW_TILE
    ntok = xp.shape[0] + xs.shape[0]
    nt = ntok // ROW_TILE
    row = lambda i, mr: (i, 0)
    const = lambda i, mr: (0, 0)
    xp_spec, xs_spec = _two_group_specs(n_prompt_tiles)
    n_heads = ATTN_WIDTH // HEAD_DIM
    per_head = lambda g: jnp.tile(g.reshape(1, HEAD_DIM), (1, n_heads))
    grid_spec = pltpu.PrefetchScalarGridSpec(
        num_scalar_prefetch=1,
        grid=(nt,),
        in_specs=[xp_spec, xs_spec,
                  pl.BlockSpec((1, D_MODEL), const),
                  pl.BlockSpec((1, 1, D_MODEL), lambda i, mr: (mr[i] * 6 + 0, 0, 0)),
                  pl.BlockSpec((1, 1, D_MODEL), lambda i, mr: (mr[i] * 6 + 1, 0, 0)),
                  pl.BlockSpec((D_MODEL, IN_WIDTH), const),
                  pl.BlockSpec((1, ATTN_WIDTH), const), pl.BlockSpec((1, ATTN_WIDTH), const),
                  pl.BlockSpec((ATTN_WIDTH, ATTN_WIDTH), const)],
        out_specs=[pl.BlockSpec((ROW_TILE, ATTN_WIDTH), row)] * 3 + [pl.BlockSpec((ROW_TILE, RWKV_IN), row)],
    )
    return pl.pallas_call(
        functools.partial(_inproj_kernel, n_prompt_tiles),
        grid_spec=grid_spec,
        out_shape=[jax.ShapeDtypeStruct((ntok, ATTN_WIDTH), F32)] * 3 + [jax.ShapeDtypeStruct((ntok, RWKV_IN), F32)],
        compiler_params=pltpu.CompilerParams(dimension_semantics=("arbitrary",), vmem_limit_bytes=VMEM_LIMIT),
        name="inproj",
    )(mrow_tile, xp, xs, g_norm1.reshape(1, -1), mod3, mod3, w_in_bf, per_head(q_norm_g), per_head(k_norm_g),
      _head_ones())


def _lane_is_head0(shape):
    return lax.broadcasted_iota(jnp.int32, shape, len(shape) - 1) < HEAD_DIM


def _attn_kernel(hasprev_ref, hasnext_ref,
                 q_ref, kp_ref, kc_ref, kx_ref, vp_ref, vc_ref, vx_ref, beta_ref, bias_ref,
                 o_ref,
                 qd, kd, vd, tq, tk, tv, acc_o, acc_m):
    i = pl.program_id(0)
    has_prev = hasprev_ref[i]
    has_next = hasnext_ref[i]
    rows = 256

    m0_o = _lane_is_head0((HALF, 128))

    for di, dil in enumerate(DILATIONS):
        lq = SEQ_BLOCK // dil
        lk = lq + 2 * HALF
        nbq = lq // HALF
        pbase = HALO - HALF * dil

        if dil == 1:
            def cast_rows(src_ref, src_off, dst_ref, dst_off, nrows, chunk):
                def body(t, carry):
                    r0 = pl.multiple_of(t * chunk, chunk)
                    dst_ref[pl.ds(dst_off + r0, chunk), :] = src_ref[pl.ds(src_off + r0, chunk), :].astype(BF16)
                    return carry
                lax.fori_loop(0, nrows // chunk, body, 0)
            cast_rows(q_ref, 0, qd, 0, lq, rows)
            for prev_ref, cur_ref, next_ref, dst in ((kp_ref, kc_ref, kx_ref, kd), (vp_ref, vc_ref, vx_ref, vd)):
                cast_rows(prev_ref, pbase, dst, 0, HALF, HALF)
                cast_rows(cur_ref, 0, dst, HALF, lq, rows)
                cast_rows(next_ref, 0, dst, HALF + lq, HALF, HALF)
        elif dil == 16:
            assert pbase == 0 and lk * dil == SEQ_BLOCK + 2 * HALO
            wq, wk = SEQ_BLOCK // 4, (SEQ_BLOCK + 2 * HALO) // 4

            def split4(r0, carry, wq=wq, wk=wk):
                tq[pl.ds(pl.multiple_of(r0 * wq, HALF), wq), :] = q_ref[pl.ds(r0, wq, stride=4), :]
                base = pl.multiple_of(r0 * wk, HALF)
                for prev_ref, cur_ref, next_ref, dst in ((kp_ref, kc_ref, kx_ref, tk), (vp_ref, vc_ref, vx_ref, tv)):
                    dst[pl.ds(base, HALO // 4), :] = prev_ref[pl.ds(r0, HALO // 4, stride=4), :]
                    dst[pl.ds(base + HALO // 4, wq), :] = cur_ref[pl.ds(r0, wq, stride=4), :]
                    dst[pl.ds(base + HALO // 4 + wq, HALO // 4), :] = next_ref[pl.ds(r0, HALO // 4, stride=4), :]
                return carry
            lax.fori_loop(0, 4, split4, 0)

            def regroup16(r, carry, lq=lq, lk=lk, wq=wq, wk=wk):
                r0 = r & 3
                r1 = r >> 2
                qd[pl.ds(pl.multiple_of(r * lq, HALF), lq), :] = tq[pl.ds(r0 * wq + r1, lq, stride=4), :].astype(BF16)
                ko = pl.multiple_of(r * lk, HALF)
                kd[pl.ds(ko, lk), :] = tk[pl.ds(r0 * wk + r1, lk, stride=4), :].astype(BF16)
                vd[pl.ds(ko, lk), :] = tv[pl.ds(r0 * wk + r1, lk, stride=4), :].astype(BF16)
                return carry
            lax.fori_loop(0, dil, regroup16, 0)
        else:
            def regroup(r, carry, dil=dil, lq=lq, lk=lk, pbase=pbase):
                qd[pl.ds(pl.multiple_of(r * lq, HALF), lq), :] = q_ref[pl.ds(r, lq, stride=dil), :].astype(BF16)
                ko = pl.multiple_of(r * lk, HALF)
                for prev_ref, cur_ref, next_ref, dst in ((kp_ref, kc_ref, kx_ref, kd), (vp_ref, vc_ref, vx_ref, vd)):
                    dst[pl.ds(ko, HALF), :] = prev_ref[pl.ds(pbase + r, HALF, stride=dil), :].astype(BF16)
                    dst[pl.ds(ko + HALF, lq), :] = cur_ref[pl.ds(r, lq, stride=dil), :].astype(BF16)
                    dst[pl.ds(ko + HALF + lq, HALF), :] = next_ref[pl.ds(r, HALF, stride=dil), :].astype(BF16)
                return carry
            lax.fori_loop(0, dil, regroup, 0)

        shift = nbq.bit_length() - 1

        def bands(it, carry, di=di, dil=dil, nbq=nbq, shift=shift):
            work = []
            for u in range(BANDS_PER_STEP):
                idx = it * BANDS_PER_STEP + u
                r = idx >> shift
                b = idx & (nbq - 1)
                qo = pl.multiple_of(idx * HALF, HALF)
                ko = pl.multiple_of(idx * HALF + r * (2 * HALF), HALF)
                variant = jnp.where(jnp.logical_and(b == 0, has_prev == 0), 1,
                                    jnp.where(jnp.logical_and(b == nbq - 1, has_next == 0), 2, 0))
                work.append((r + (dil * HALF) * b, di * 3 + variant, qd[pl.ds(qo, HALF), :],
                             kd[pl.ds(ko, 3 * HALF), :], vd[pl.ds(ko, 3 * HALF), :]))
            s, m, l, o = [], [], [], []
            for w in work:
                si = _dot_nt(_stack_heads(w[2]), w[3]) + bias_ref[w[1]]
                s.append(si)
                m.append(jnp.max(si, axis=-1, keepdims=True))
            for w, si, mi in zip(work, s, m):
                pi = jnp.exp2(si - mi)
                l.append(jnp.sum(pi, axis=-1, keepdims=True))
                o.append(_dot(pi.astype(BF16), w[4]))
            for w, oi, mi, li in zip(work, o, m, l):
                if dil == 1:
                    dst = pl.ds(pl.multiple_of(w[0], HALF), HALF)
                else:
                    dst = pl.ds(w[0], HALF, stride=dil)
                on = oi * (1.0 / li)
                lse = mi + jnp.log2(li)
                acc_o[di][dst, :] = jnp.where(m0_o, on[0:HALF], on[HALF:2 * HALF])
                acc_m[di][dst, :] = jnp.where(m0_o, lse[0:HALF], lse[HALF:2 * HALF])
            return carry

        lax.fori_loop(0, SEQ_BLOCK // HALF // BANDS_PER_STEP, bands, 0)

    beta = beta_ref[...]

    def merge(t, carry):
        r0 = pl.multiple_of(t * rows, rows)
        sl = pl.ds(r0, rows)
        ms = [acc_m[d][sl, :] for d in range(3)]
        mx = jnp.maximum(jnp.maximum(ms[0], ms[1]), ms[2])
        num = jnp.zeros((rows, 128), F32)
        den = jnp.zeros((rows, 128), F32)
        for d in range(3):
            e = jnp.exp2(ms[d] - mx)
            num = num + e * acc_o[d][sl, :]
            den = den + e
        o_ref[sl, :] = ((num / den) * beta).astype(o_ref.dtype)
        return carry

    lax.fori_loop(0, SEQ_BLOCK // rows, merge, 0)


def _attn_bias_table():
    qi = np.arange(HALF)[:, None]
    kj = np.arange(3 * HALF)[None, :]
    rel = np.abs(kj - HALF - qi).astype(np.float32)
    n_heads = ATTN_WIDTH // HEAD_DIM
    slopes = 2.0 ** (-8.0 * (np.arange(n_heads, dtype=np.float32) + 1.0) / n_heads)
    keep = [rel <= HALF, (rel <= HALF) & (kj >= HALF), (rel <= HALF) & (kj < 2 * HALF)]
    tab = np.empty((N_PAIRS, 3, 3, 2, HALF, 3 * HALF), np.float32)
    for h in range(n_heads):
        for di, dil in enumerate(DILATIONS):
            for var in range(3):
                tab[h // 2, di, var, h % 2] = np.where(keep[var], -(slopes[h] * (dil * rel)) * LOG2_E, NEG_INF)
    return jnp.asarray(tab.reshape(N_PAIRS * 9, 2 * HALF, 3 * HALF))


def _attention(q, k, v, has_prev, has_next, attn_beta):
    ntok = q.shape[0]
    nb = ntok // SEQ_BLOCK
    hb = SEQ_BLOCK // HALO
    cur = lambda i, j, hp, hn: (i, j)
    prev = lambda i, j, hp, hn: (jnp.maximum(i * hb - 1, 0), j)
    nxt = lambda i, j, hp, hn: (jnp.minimum((i + 1) * hb, nb * hb - 1), j)
    blk = pl.BlockSpec((SEQ_BLOCK, 128), cur)
    halo_p = pl.BlockSpec((HALO, 128), prev)
    halo_n = pl.BlockSpec((HALO, 128), nxt)
    win = SEQ_BLOCK + 2 * HALO
    grid_spec = pltpu.PrefetchScalarGridSpec(
        num_scalar_prefetch=2,
        grid=(nb, N_PAIRS),
        in_specs=[blk, halo_p, blk, halo_n, halo_p, blk, halo_n,
                  pl.BlockSpec((1, 128), lambda i, j, hp, hn: (0, j)),
                  pl.BlockSpec((9, 2 * HALF, 3 * HALF), lambda i, j, hp, hn: (j, 0, 0))],
        out_specs=blk,
        scratch_shapes=[pltpu.VMEM((SEQ_BLOCK, 128), BF16), pltpu.VMEM((win, 128), BF16), pltpu.VMEM((win, 128), BF16),
                        pltpu.VMEM((SEQ_BLOCK, 128), F32), pltpu.VMEM((win, 128), F32), pltpu.VMEM((win, 128), F32),
                        [pltpu.VMEM((SEQ_BLOCK, 128), F32)] * 3,
                        [pltpu.VMEM((SEQ_BLOCK, 128), F32)] * 3],
    )
    return pl.pallas_call(
        _attn_kernel,
        grid_spec=grid_spec,
        out_shape=jax.ShapeDtypeStruct((ntok, ATTN_WIDTH), BF16),
        compiler_params=pltpu.CompilerParams(dimension_semantics=("arbitrary", "arbitrary"),
                                             vmem_limit_bytes=VMEM_LIMIT),
        name="attn",
    )(has_prev, has_next, q, k, k, k, v, v, v, attn_beta.reshape(1, -1), _attn_bias_table())


def _prep_kernel(first_ref, last_ref,
                 p_ref, hp_ref, hn_ref, mup_ref, mun_ref, w0_ref, wup_ref, a0_ref, aup_ref, gup_ref,
                 kk_ref, ka_ref, rk_ref, ones_ref,
                 r_out, kk_out, v_out, g_out, bonus_out, lw_out, kd_out, bd_out):
    i = pl.program_id(0)
    tm = p_ref.shape[0]
    p = p_ref[...]
    row = lax.broadcasted_iota(jnp.int32, (tm, 1), 0)
    keep_prev = jnp.where(first_ref[i] == 0, 1.0, 0.0)
    keep_next = jnp.where(last_ref[i] == 0, 1.0, 0.0)
    prev = jnp.where(row == 0, hp_ref[7:8, :] * keep_prev, pltpu.roll(p, 1, 0))
    nxt = jnp.where(row == tm - 1, hn_ref[0:1, :] * keep_next, pltpu.roll(p, tm - 1, 0))
    ps = p + mup_ref[...] * (prev - p) + mun_ref[...] * (nxt - p)

    c = RWKV_WIDTH
    r = ps[:, 0:c]
    k = ps[:, c:2 * c]
    v = ps[:, 2 * c:3 * c]
    wd = ps[:, 3 * c:3 * c + 128]
    ad = ps[:, 3 * c + 128:3 * c + 256]
    gd = ps[:, 3 * c + 256:3 * c + 384]

    w_raw = w0_ref[...] + _dot(jnp.tanh(wd).astype(BF16), wup_ref[...])
    lw = (-float(np.exp(-0.5))) * _sigmoid(w_raw)
    a = _sigmoid(a0_ref[...] + _dot(ad.astype(BF16), aup_ref[...]))
    g = _dot(_sigmoid(gd).astype(BF16), gup_ref[...])

    ones_bd = ones_ref[...]
    kkv = k * kk_ref[...]
    ssq = _dot((kkv * kkv).astype(BF16), ones_bd)
    kkv = kkv * lax.rsqrt(jnp.maximum(ssq, 1e-24))
    ka = ka_ref[...]
    kd0 = k * (1.0 + (a[:, 0:c] - 1.0) * ka)
    kd1 = k * (1.0 + (a[:, c:2 * c] - 1.0) * ka)
    hsum = _dot((r * (kd0 + kd1) * rk_ref[...]).astype(BF16), ones_bd)

    r_out[...] = r.astype(BF16)
    kk_out[...] = kkv.astype(BF16)
    v_out[...] = v.astype(BF16)
    g_out[...] = g
    bonus_out[...] = hsum * v
    lw_out[0] = lw[:, 0:c]
    lw_out[1] = lw[:, c:2 * c]
    kd_out[0] = kd0.astype(BF16)
    kd_out[1] = kd1.astype(BF16)
    bd_out[0] = (kkv * a[:, 0:c]).astype(BF16)
    bd_out[1] = (kkv * a[:, c:2 * c]).astype(BF16)


def _block_diag2(m):
    z = jnp.zeros_like(m[0])
    return jnp.concatenate([jnp.concatenate([m[0], z], axis=1), jnp.concatenate([z, m[1]], axis=1)], axis=0)


def _head_ones():
    idx = np.arange(RWKV_WIDTH) // HEAD_DIM
    return jnp.asarray((idx[:, None] == idx[None, :]).astype(np.float32), dtype=BF16)


def _prep(p_rw, first_tile, last_tile, mu_prev, mu_next, w0, w_up, a0, a_up, g_up, k_k, k_a, r_k):
    ntok = p_rw.shape[0]
    tm = PREP_TILE
    nt = ntok // tm
    c = RWKV_WIDTH
    row = lambda i, f, l: (i, 0)
    const = lambda i, f, l: (0, 0)
    grid_spec = pltpu.PrefetchScalarGridSpec(
        num_scalar_prefetch=2,
        grid=(nt,),
        in_specs=[pl.BlockSpec((tm, RWKV_IN), row),
                  pl.BlockSpec((8, RWKV_IN), lambda i, f, l: (jnp.maximum(i * (tm // 8) - 1, 0), 0)),
                  pl.BlockSpec((8, RWKV_IN), lambda i, f, l: (jnp.minimum((i + 1) * (tm // 8), ntok // 8 - 1), 0)),
                  pl.BlockSpec((1, RWKV_IN), const), pl.BlockSpec((1, RWKV_IN), const),
                  pl.BlockSpec((1, 2 * c), const), pl.BlockSpec((128, 2 * c), const),
                  pl.BlockSpec((1, 2 * c), const), pl.BlockSpec((128, 2 * c), const),
                  pl.BlockSpec((128, c), const),
                  pl.BlockSpec((1, c), const), pl.BlockSpec((1, c), const), pl.BlockSpec((1, c), const),
                  pl.BlockSpec((c, c), const)],
        out_specs=[pl.BlockSpec((tm, c), row)] * 5 + [pl.BlockSpec((2, tm, c), lambda i, f, l: (0, i, 0))] * 3,
    )
    tok = lambda dt: jax.ShapeDtypeStruct((ntok, c), dt)
    tok2 = lambda dt: jax.ShapeDtypeStruct((2, ntok, c), dt)
    return pl.pallas_call(
        _prep_kernel,
        grid_spec=grid_spec,
        out_shape=[tok(BF16), tok(BF16), tok(BF16), tok(F32), tok(F32), tok2(F32), tok2(BF16), tok2(BF16)],
        compiler_params=pltpu.CompilerParams(dimension_semantics=("arbitrary",), vmem_limit_bytes=VMEM_LIMIT),
        name="rwkv_prep",
    )(first_tile, last_tile, p_rw, p_rw, p_rw, mu_prev.reshape(1, -1), mu_next.reshape(1, -1),
      w0.reshape(1, -1), _block_diag2(w_up).astype(BF16), a0.reshape(1, -1), _block_diag2(a_up).astype(BF16),
      g_up.astype(BF16), k_k.reshape(1, -1), k_a.reshape(1, -1), r_k.reshape(1, -1), _head_ones())


def _stack_heads(x):
    m0 = _lane_is_head0(x.shape)
    z = jnp.zeros_like(x)
    return jnp.concatenate([jnp.where(m0, x, z), jnp.where(m0, z, x)], axis=0)


def _stack_heads_bf16(x):
    return _stack_heads(x.astype(BF16))


def _scan_masks(rev):
    c = CHUNK
    rowi = lax.broadcasted_iota(jnp.int32, (c, 128), 0)
    colj = lax.broadcasted_iota(jnp.int32, (c, 128), 1) & (c - 1)
    ti = lax.broadcasted_iota(jnp.int32, (c, c), 0)
    tj = lax.broadcasted_iota(jnp.int32, (c, c), 1)
    if rev:
        strict, incl, tri = colj > rowi, colj >= rowi, tj >= ti
    else:
        strict, incl, tri = colj < rowi, colj <= rowi, tj <= ti
    return dict(strict=strict, incl=incl, tri=tri.astype(BF16), eye2=(colj == rowi).astype(F32))


def _scan_pre(ins, masks):
    c = CHUNK
    idx = range(len(ins))
    lw = [a[0] for a in ins]
    v = [a[3] for a in ins]

    cl = []
    for i in idx:
        hi, mid, lo = _split3(lw[i])
        cl3 = _dot(masks[i]["tri"], jnp.concatenate([hi, mid, lo], axis=1))
        cl.append(cl3[:, 0:128] + cl3[:, 128:256] + cl3[:, 256:384])
    yield
    g_tot = [jnp.exp(cl[i][0:1, :] if masks[i]["rev"] else cl[i][c - 1:c, :]) for i in idx]
    g_inv = [jnp.exp(-cl[i]) for i in idx]
    rt = [ins[i][1] * jnp.exp(cl[i]) for i in idx]
    bt = [ins[i][2] * jnp.exp(cl[i] - lw[i]) for i in idx]
    at = [-(ins[i][5] * g_inv[i]) for i in idx]
    kt = [ins[i][4] * g_inv[i] for i in idx]
    br = [jnp.concatenate([bt[i], rt[i]], axis=0).astype(BF16) for i in idx]
    lm = [_dot_nt(br[i], jnp.concatenate([_stack_heads_bf16(at[i]), _stack_heads_bf16(kt[i])], axis=0))
          for i in idx]
    akg = [jnp.concatenate([at[i] * g_tot[i], kt[i] * g_tot[i]], axis=0).astype(BF16) for i in idx]
    yield
    la = [jnp.where(masks[i]["strict"], lm[i][0:c, 0:128], 0.0) for i in idx]
    lk = [jnp.where(masks[i]["strict"], lm[i][0:c, 128:256], 0.0).astype(BF16) for i in idx]
    mam = [jnp.concatenate([jnp.where(masks[i]["incl"], lm[i][c:2 * c, 0:128], 0.0),
                            jnp.where(masks[i]["incl"], lm[i][c:2 * c, 128:256], 0.0)], axis=1).astype(BF16)
           for i in idx]

    sv = [_stack_heads_bf16(v[i]) for i in idx]
    lkv = [_dot(lk[i], sv[i]) for i in idx]

    t = [masks[i]["eye2"] + la[i] for i in idx]
    pw = [_dot(la[i].astype(BF16), _stack_heads_bf16(la[i])) for i in idx]
    yield
    for step in range(5):
        if step < 4:
            out = [_dot(pw[i].astype(BF16),
                        jnp.concatenate([_stack_heads_bf16(pw[i]), _stack_heads_bf16(t[i])], axis=1))
                   for i in idx]
            t = [t[i] + out[i][:, 128:256] for i in idx]
            pw = [out[i][:, 0:128] for i in idx]
        else:
            t = [t[i] + _dot(pw[i].astype(BF16), _stack_heads_bf16(t[i])) for i in idx]
        yield
    return [dict(br=br[i], lkv=lkv[i], mam=mam[i], sv=sv[i], t=t[i].astype(BF16), v=v[i].astype(BF16),
                 akg=akg[i], g_tot=g_tot[i]) for i in idx]


def _scan_seq(pre, zs, n_chunks):
    c = CHUNK
    per = len(zs)
    idx = range(per)
    zr = lax.broadcasted_iota(jnp.int32, (128, 128), 0) < HEAD_DIM
    zc = lax.broadcasted_iota(jnp.int32, (128, 128), 1) < HEAD_DIM
    same_head = zr == zc
    ys = []
    for k in range(n_chunks):
        p = pre[k * per:(k + 1) * per]
        brz = [_dot_nt(p[i]["br"], zs[i].astype(BF16)) for i in idx]
        yield
        u = [_dot(p[i]["t"], _stack_heads_bf16(brz[i][0:c] + p[i]["lkv"])) for i in idx]
        yield
        ys += [brz[i][c:2 * c] + _dot(p[i]["mam"], jnp.concatenate([_stack_heads_bf16(u[i]), p[i]["sv"]], axis=0))
               for i in idx]
        zs = [zs[i] * p[i]["g_tot"]
              + jnp.where(same_head, _dot_tn(jnp.concatenate([u[i].astype(BF16), p[i]["v"]], axis=0), p[i]["akg"]), 0.0)
              for i in idx]
        yield
    return ys, zs


def _run(gen):
    try:
        while True:
            next(gen)
    except StopIteration as done:
        return done.value


def _interleave(gen_a, gen_b):
    live = {"a": gen_a, "b": gen_b}
    result = {}
    while live:
        for name in list(live):
            try:
                next(live[name])
            except StopIteration as done:
                result[name] = done.value
                del live[name]
    return result["a"], result["b"]


def _scan_kernel(bmap_ref, reset_ref,
                 rf, kkf, vf, lwf, kdf, bdf, rb, kkb, vb, lwb, kdb, bdb,
                 yf_ref, yb_ref, zf, zb):
    del bmap_ref
    g = pl.program_id(0)

    @pl.when(reset_ref[g] == 1)
    def _():
        zf[...] = jnp.zeros_like(zf)
        zb[...] = jnp.zeros_like(zb)

    nchunk = SCAN_BLOCK // CHUNK
    n_groups = nchunk // CHUNKS_PER_GROUP
    mask_f = dict(_scan_masks(False), rev=False)
    mask_b = dict(_scan_masks(True), rev=True)
    dirs = ((mask_f, (lwf, rf, kkf, vf, kdf, bdf), yf_ref, zf),
            (mask_b, (lwb, rb, kkb, vb, kdb, bdb), yb_ref, zb))

    def group_items(gi):
        ins, masks, outs = [], [], []
        for k in range(CHUNKS_PER_GROUP):
            step = gi * CHUNKS_PER_GROUP + k
            for mask, refs, y_ref, _ in dirs:
                cc = (nchunk - 1 - step) if mask["rev"] else step
                rows = slice(cc * CHUNK, (cc + 1) * CHUNK)
                for pr in range(N_PAIRS):
                    lanes = slice(128 * pr, 128 * (pr + 1))
                    ins.append([ref[rows, lanes] for ref in refs])
                    masks.append(mask)
                    outs.append((y_ref, rows, lanes))
        return ins, masks, outs

    zs = [z_ref[pr] for _, _, _, z_ref in dirs for pr in range(N_PAIRS)]
    ins, masks, outs = group_items(0)
    pre = _run(_scan_pre(ins, masks))
    for gi in range(n_groups):
        seq = _scan_seq(pre, zs, CHUNKS_PER_GROUP)
        if gi + 1 < n_groups:
            ins, masks, next_outs = group_items(gi + 1)
            pre, (ys, zs) = _interleave(_scan_pre(ins, masks), seq)
        else:
            ys, zs = _run(seq)
        for (y_ref, rows, lanes), y in zip(outs, ys):
            y_ref[rows, lanes] = y
        if gi + 1 < n_groups:
            outs = next_outs
    state_refs = [(z_ref, pr) for _, _, _, z_ref in dirs for pr in range(N_PAIRS)]
    for (z_ref, pr), z in zip(state_refs, zs):
        z_ref[pr] = z


def _scan(r, kk, v, lw, kd, bd, bmap_bwd, reset):
    ntok = r.shape[0]
    c = RWKV_WIDTH
    ns = ntok // SCAN_BLOCK
    fwd = lambda g, bm, rs: (g, 0)
    bwd = lambda g, bm, rs: (bm[g], 0)
    fwd2 = lambda g, bm, rs: (0, g, 0)
    bwd2 = lambda g, bm, rs: (1, bm[g], 0)
    tokf = pl.BlockSpec((SCAN_BLOCK, c), fwd)
    tokb = pl.BlockSpec((SCAN_BLOCK, c), bwd)
    dirf = pl.BlockSpec((None, SCAN_BLOCK, c), fwd2)
    dirb = pl.BlockSpec((None, SCAN_BLOCK, c), bwd2)
    grid_spec = pltpu.PrefetchScalarGridSpec(
        num_scalar_prefetch=2,
        grid=(ns,),
        in_specs=[tokf, tokf, tokf, dirf, dirf, dirf, tokb, tokb, tokb, dirb, dirb, dirb],
        out_specs=[tokf, tokb],
        scratch_shapes=[pltpu.VMEM((N_PAIRS, 128, 128), F32), pltpu.VMEM((N_PAIRS, 128, 128), F32)],
    )
    tok = jax.ShapeDtypeStruct((ntok, c), F32)
    return pl.pallas_call(
        _scan_kernel,
        grid_spec=grid_spec,
        out_shape=[tok, tok],
        compiler_params=pltpu.CompilerParams(dimension_semantics=("arbitrary",), vmem_limit_bytes=VMEM_LIMIT),
        name="rwkv_scan",
    )(bmap_bwd, reset, r, kk, v, lw, kd, bd, r, kk, v, lw, kd, bd)


def _post_ffn_kernel(n_prompt_tiles, mrow_ref, xp_ref, xs_ref, yf_ref, yb_ref, bonus_ref, g_ref, attn_ref,
                     gt1_ref, sh2_ref, sc2_ref, gt2_ref, g2_ref, lnw_ref, lnb_ref, ones_ref, wo_ref, w1_ref, w2_ref,
                     op_ref, os_ref):
    del mrow_ref
    is_prompt = pl.program_id(0) < n_prompt_tiles
    x = jnp.where(is_prompt, xp_ref[...], xs_ref[...])

    y = yf_ref[...] + yb_ref[...]
    ones_bd = ones_ref[...]
    y1, y2, _ = _split3(y)
    mu = (_dot(y1, ones_bd) + _dot(y2, ones_bd)) * (1.0 / HEAD_DIM)
    d = y - mu
    var = _dot((d * d).astype(BF16), ones_bd) * (1.0 / HEAD_DIM)
    yn = d * lax.rsqrt(var + LN_X_EPS) * lnw_ref[...] + lnb_ref[...]
    rw = ((yn + bonus_ref[...]) * g_ref[...]).astype(BF16)

    a = ATTN_WIDTH
    mix = _dot(attn_ref[...], wo_ref[0:a, :]) + _dot(rw, wo_ref[a:, :])
    x1 = x + gt1_ref[0] * mix
    h = _modulated_rmsnorm(x1, g2_ref[...], sc2_ref[0], sh2_ref[0]).astype(BF16)

    acc = jnp.zeros((h.shape[0], D_MODEL), F32)
    for j in range(D_FF // FF_STEP):
        cols = slice(j * FF_STEP, (j + 1) * FF_STEP)
        act = jnp.maximum(_dot(h, w1_ref[:, cols]), 0.0)
        acc = acc + _dot((act * act).astype(BF16), w2_ref[cols, :])
    out = x1 + gt2_ref[0] * acc

    @pl.when(is_prompt)
    def _():
        op_ref[...] = out

    @pl.when(jnp.logical_not(is_prompt))
    def _():
        os_ref[...] = out


def _post_ffn(xp, xs, yf, yb, bonus, g, attn, mod3, mrow_tile, g_norm2, ln_x_w, ln_x_b, w_out_bf, w1_bf, w2_bf):
    n_prompt_tiles = xp.shape[0] // ROW_TILE
    nt = (xp.shape[0] + xs.shape[0]) // ROW_TILE
    c = RWKV_WIDTH
    row = lambda i, mr: (i, 0)
    const = lambda i, mr: (0, 0)
    resident = lambda shape: pl.BlockSpec(shape, const, pipeline_mode=pl.Buffered(1))
    modspec = lambda j: pl.BlockSpec((1, 1, D_MODEL), lambda i, mr: (mr[i] * 6 + j, 0, 0))
    xp_spec, xs_spec = _two_group_specs(n_prompt_tiles)
    half = pl.BlockSpec((ROW_TILE, c), row)
    grid_spec = pltpu.PrefetchScalarGridSpec(
        num_scalar_prefetch=1,
        grid=(nt,),
        in_specs=[xp_spec, xs_spec, half, half, half, half, half,
                  modspec(2), modspec(3), modspec(4), modspec(5),
                  pl.BlockSpec((1, D_MODEL), const), pl.BlockSpec((1, c), const), pl.BlockSpec((1, c), const),
                  resident((c, c)), resident((D_MODEL, D_MODEL)), resident((D_MODEL, D_FF)),
                  resident((D_FF, D_MODEL))],
        out_specs=[xp_spec, xs_spec],
    )
    return pl.pallas_call(
        functools.partial(_post_ffn_kernel, n_prompt_tiles),
        grid_spec=grid_spec,
        out_shape=[jax.ShapeDtypeStruct(xp.shape, F32), jax.ShapeDtypeStruct(xs.shape, F32)],
        compiler_params=pltpu.CompilerParams(dimension_semantics=("arbitrary",), vmem_limit_bytes=VMEM_LIMIT),
        name="post_ffn",
    )(mrow_tile, xp, xs, yf, yb, bonus, g, attn, mod3, mod3, mod3, mod3, g_norm2.reshape(1, -1),
      ln_x_w.reshape(1, -1), ln_x_b.reshape(1, -1), _head_ones(), w_out_bf, w1_bf, w2_bf)


def _sequence_tables(seq_blocks):
    mod_row, has_prev, has_next = [], [], []
    for s, n in enumerate(seq_blocks):
        for b in range(n):
            mod_row.append(s)
            has_prev.append(int(b > 0))
            has_next.append(int(b < n - 1))
    per = SEQ_BLOCK // SCAN_BLOCK
    bmap_bwd, reset = [], []
    start = 0
    for n in seq_blocks:
        ns = n * per
        for t in range(ns):
            bmap_bwd.append(start + ns - 1 - t)
            reset.append(int(t == 0))
        start += ns
    i32 = lambda a: jnp.asarray(np.asarray(a, np.int32))
    rep = lambda a, k: np.repeat(np.asarray(a, np.int32), k)
    row_tiles = SEQ_BLOCK // ROW_TILE
    prep_tiles = SEQ_BLOCK // PREP_TILE
    first = np.zeros(len(mod_row) * prep_tiles, np.int32)
    last = np.zeros(len(mod_row) * prep_tiles, np.int32)
    for b in range(len(mod_row)):
        if not has_prev[b]:
            first[b * prep_tiles] = 1
        if not has_next[b]:
            last[(b + 1) * prep_tiles - 1] = 1
    return dict(mrow_tile=i32(rep(mod_row, row_tiles)), has_prev=i32(has_prev), has_next=i32(has_next),
                bmap_bwd=i32(bmap_bwd), reset=i32(reset), first=i32(first), last=i32(last))


def _layer(xp, xs, c_all, tabs, w_ada, b_ada, g_norm1, g_norm2, w_in, q_norm_g, k_norm_g, attn_beta,
           mu_prev, mu_next, w0, w_up, a0, a_up, g_up, k_k, k_a, r_k, ln_x_w, ln_x_b, w_out, w_ff1, w_ff2):
    mod = _ada(c_all, w_ada, b_ada)
    mod3 = mod.reshape(-1, 1, D_MODEL)
    q, k, v, p_rw = _inproj(xp, xs, mod3, tabs["mrow_tile"], g_norm1, w_in.astype(BF16), q_norm_g, k_norm_g)
    attn = _attention(q, k, v, tabs["has_prev"], tabs["has_next"], attn_beta)
    r, kk, vv, g, bonus, lw, kd, bd = _prep(p_rw, tabs["first"], tabs["last"], mu_prev, mu_next, w0, w_up,
                                            a0, a_up, g_up, k_k, k_a, r_k)
    yf, yb = _scan(r, kk, vv, lw, kd, bd, tabs["bmap_bwd"], tabs["reset"])
    return _post_ffn(xp, xs, yf, yb, bonus, g, attn, mod3, tabs["mrow_tile"], g_norm2, ln_x_w, ln_x_b,
                     w_out.astype(BF16), w_ff1.astype(BF16), w_ff2.astype(BF16))


def kernel(x_prompt, x_sample, c_prompt, c_sample, w_ada, b_ada, g_norm1, g_norm2, w_in, q_norm_g, k_norm_g, attn_beta, mu_prev, mu_next, w0, w_up, a0, a_up, g_up, k_k, k_a, r_k, ln_x_w, ln_x_b, w_out, w_ff1, w_ff2):
    bp, sp, d = x_prompt.shape
    bs, ss, _ = x_sample.shape
    assert d == D_MODEL and sp % SEQ_BLOCK == 0 and ss % SEQ_BLOCK == 0
    seq_blocks = [sp // SEQ_BLOCK] * bp + [ss // SEQ_BLOCK] * bs
    tabs = _sequence_tables(seq_blocks)
    n_seq = bp + bs
    pad = (-n_seq) % 8
    c_all = jnp.concatenate([c_prompt, c_sample, jnp.zeros((pad, d), F32)], axis=0)
    yp = x_prompt.reshape(bp * sp, d)
    ys = x_sample.reshape(bs * ss, d)
    for i in range(w_ada.shape[0]):
        yp, ys = _layer(yp, ys, c_all, tabs, w_ada[i], b_ada[i], g_norm1[i], g_norm2[i], w_in[i], q_norm_g[i],
                        k_norm_g[i], attn_beta[i], mu_prev[i], mu_next[i], w0[i], w_up[i], a0[i], a_up[i], g_up[i],
                        k_k[i], k_a[i], r_k[i], ln_x_w[i], ln_x_b[i], w_out[i], w_ff1[i], w_ff2[i])
    return (yp.reshape(bp, sp, d), ys.reshape(bs, ss, d))
```

```python
import functools

import numpy as np
import jax
import jax.numpy as jnp
from jax import lax
from jax.experimental import pallas as pl
from jax.experimental.pallas import tpu as pltpu

F32 = jnp.float32
BF16 = jnp.bfloat16

D_MODEL = 1024
HEAD_DIM = 64
ATTN_WIDTH = 512
RWKV_WIDTH = 512
N_PAIRS = ATTN_WIDTH // 128
RWKV_IN = 1920
IN_WIDTH = 3 * ATTN_WIDTH + RWKV_IN
D_FF = 4096
DILATIONS = (1, 4, 16)
HALF = 64
NORM_EPS = 1e-6
LN_X_EPS = 64e-5
NEG_INF = -1e30
LOG2_E = 1.4426950408889634

SEQ_BLOCK = 2048
HALO = 1024
SCAN_BLOCK = 512
CHUNK = 64
CHUNKS_PER_GROUP = 2
ROW_TILE = 512
BANDS_PER_STEP = 16
FF_STEP = 1024

VMEM_LIMIT = 56 * 1024 * 1024


def _dot(a, b):
    return jnp.dot(a, b, preferred_element_type=F32)


def _dot_nt(a, b):
    return lax.dot_general(a, b, (((1,), (1,)), ((), ())), preferred_element_type=F32)


def _dot_tn(a, b):
    return lax.dot_general(a, b, (((0,), (0,)), ((), ())), preferred_element_type=F32)


def _split3(x):
    hi = x.astype(BF16)
    r1 = x - hi.astype(F32)
    mid = r1.astype(BF16)
    lo = (r1 - mid.astype(F32)).astype(BF16)
    return hi, mid, lo


def _sigmoid(x):
    return 1.0 / (1.0 + jnp.exp(-x))


def _ada_kernel(c_ref, w_ref, b_ref, o_ref):
    c = c_ref[...]
    s = c * _sigmoid(c)
    s1, s2, _ = _split3(s)
    w = w_ref[...]
    w1, w2, _ = _split3(w)
    o_ref[...] = _dot(s1, w1) + (_dot(s1, w2) + _dot(s2, w1)) + b_ref[...]


def _ada(c_all, w_ada, b_ada):
    n = c_all.shape[0]
    nt = w_ada.shape[1] // D_MODEL
    return pl.pallas_call(
        _ada_kernel,
        grid=(nt,),
        in_specs=[pl.BlockSpec((n, D_MODEL), lambda j: (0, 0)),
                  pl.BlockSpec((D_MODEL, D_MODEL), lambda j: (0, j)),
                  pl.BlockSpec((1, D_MODEL), lambda j: (0, j))],
        out_specs=pl.BlockSpec((n, D_MODEL), lambda j: (0, j)),
        out_shape=jax.ShapeDtypeStruct((n, w_ada.shape[1]), F32),
        compiler_params=pltpu.CompilerParams(dimension_semantics=("arbitrary",), vmem_limit_bytes=VMEM_LIMIT),
        name="ada",
    )(c_all, w_ada, b_ada.reshape(1, -1))


def _modulated_rmsnorm(x, g, scale, shift):
    ms = jnp.mean(x * x, axis=-1, keepdims=True)
    return (x * lax.rsqrt(ms + NORM_EPS) * g) * (1.0 + scale) + shift


def _two_group_specs(n_prompt_tiles):
    def prompt(i, *_):
        return (jnp.minimum(i, n_prompt_tiles - 1), 0)

    def sample(i, *_):
        return (jnp.maximum(i - n_prompt_tiles, 0), 0)
    return pl.BlockSpec((ROW_TILE, D_MODEL), prompt), pl.BlockSpec((ROW_TILE, D_MODEL), sample)


def _head_rms_scale(z, ones_bd):
    ssq = _dot((z * z).astype(BF16), ones_bd)
    return lax.rsqrt(ssq * (1.0 / HEAD_DIM) + NORM_EPS)


def _inproj_kernel(n_prompt_tiles, mrow_ref, first_ref, last_ref,
                   xp_ref, xs_ref, xpp_ref, xpn_ref, xsp_ref, xsn_ref,
                   g_ref, sh_ref, sc_ref, w_ref, qg_ref, kg_ref, ones_ref,
                   mup_ref, mun_ref, w0_ref, wup_ref, a0_ref, aup_ref, gup_ref, kk_ref, ka_ref, rk_ref,
                   q_ref, k_ref, v_ref,
                   r_out, kk_out, v_out, g_out, bonus_out, lw_out, kd_out, bd_out):
    del mrow_ref
    i = pl.program_id(0)
    is_prompt = i < n_prompt_tiles
    g1, sc, sh = g_ref[...], sc_ref[0], sh_ref[0]
    keep_prev = jnp.where(first_ref[i] == 0, 1.0, 0.0)
    keep_next = jnp.where(last_ref[i] == 0, 1.0, 0.0)
    h = _modulated_rmsnorm(jnp.where(is_prompt, xp_ref[...], xs_ref[...]), g1, sc, sh)
    h_prev = _modulated_rmsnorm(jnp.where(is_prompt, xpp_ref[...], xsp_ref[...]), g1, sc, sh) * keep_prev
    h_next = _modulated_rmsnorm(jnp.where(is_prompt, xpn_ref[...], xsn_ref[...]), g1, sc, sh) * keep_next

    a = ATTN_WIDTH
    hb = h.astype(BF16)
    ones_bd = ones_ref[...]
    q = _dot(hb, w_ref[:, 0:a])
    q_ref[...] = q * _head_rms_scale(q, ones_bd) * (qg_ref[...] * (HEAD_DIM ** -0.5 * LOG2_E))
    k = _dot(hb, w_ref[:, a:2 * a])
    k_ref[...] = k * _head_rms_scale(k, ones_bd) * kg_ref[...]
    v_ref[...] = _dot(hb, w_ref[:, 2 * a:3 * a])

    tm = h.shape[0]
    hext = jnp.concatenate([h_prev, h, h_next], axis=0).astype(BF16)
    next_amt = hext.shape[0] - 1

    def shifted(lo, hi):
        p = _dot(hext, w_ref[:, 3 * a + lo:3 * a + hi])
        prev = pltpu.roll(p, 1, 0)
        nxt = pltpu.roll(p, next_amt, 0)
        ps = p + mup_ref[:, lo:hi] * (prev - p) + mun_ref[:, lo:hi] * (nxt - p)
        return ps[8:8 + tm]

    c = RWKV_WIDTH
    r = shifted(0, c)
    k = shifted(c, 2 * c)
    v = shifted(2 * c, 3 * c)
    low = shifted(3 * c, 3 * c + 384)
    wd = low[:, 0:128]
    ad = low[:, 128:256]
    gd = low[:, 256:384]

    w_raw = w0_ref[...] + _dot(jnp.tanh(wd).astype(BF16), wup_ref[...])
    lw = (-float(np.exp(-0.5))) * _sigmoid(w_raw)
    a_rate = _sigmoid(a0_ref[...] + _dot(ad.astype(BF16), aup_ref[...]))
    gate = _dot(_sigmoid(gd).astype(BF16), gup_ref[...])

    kkv = k * kk_ref[...]
    ssq = _dot((kkv * kkv).astype(BF16), ones_bd)
    kkv = kkv * lax.rsqrt(jnp.maximum(ssq, 1e-24))
    ka = ka_ref[...]
    kd0 = k * (1.0 + (a_rate[:, 0:c] - 1.0) * ka)
    kd1 = k * (1.0 + (a_rate[:, c:2 * c] - 1.0) * ka)
    hsum = _dot((r * (kd0 + kd1) * rk_ref[...]).astype(BF16), ones_bd)

    r_out[...] = r.astype(BF16)
    kk_out[...] = kkv.astype(BF16)
    v_out[...] = v.astype(BF16)
    g_out[...] = gate
    bonus_out[...] = hsum * v
    lw_out[0] = lw[:, 0:c]
    lw_out[1] = lw[:, c:2 * c]
    kd_out[0] = kd0.astype(BF16)
    kd_out[1] = kd1.astype(BF16)
    bd_out[0] = (kkv * a_rate[:, 0:c]).astype(BF16)
    bd_out[1] = (kkv * a_rate[:, c:2 * c]).astype(BF16)


def _inproj(xp, xs, mod3, tabs, g_norm1, w_in_bf, q_norm_g, k_norm_g,
            mu_prev, mu_next, w0, w_up, a0, a_up, g_up, k_k, k_a, r_k):
    n_prompt_tiles = xp.shape[0] // ROW_TILE
    ntok = xp.shape[0] + xs.shape[0]
    nt = ntok // ROW_TILE
    c = RWKV_WIDTH
    per8 = ROW_TILE // 8
    row = lambda i, *_: (i, 0)
    const = lambda i, *_: (0, 0)
    resident = lambda shape: pl.BlockSpec(shape, const, pipeline_mode=pl.Buffered(1))
    xp_spec, xs_spec = _two_group_specs(n_prompt_tiles)

    def halo(n_rows, tile_of, side):
        def index(i, *_):
            t = tile_of(i)
            if side < 0:
                return (jnp.maximum(t * per8 - 1, 0), 0)
            return (jnp.minimum((t + 1) * per8, n_rows // 8 - 1), 0)
        return pl.BlockSpec((8, D_MODEL), index)

    prompt_tile = lambda i: jnp.minimum(i, n_prompt_tiles - 1)
    sample_tile = lambda i: jnp.maximum(i - n_prompt_tiles, 0)
    n_heads = ATTN_WIDTH // HEAD_DIM
    per_head = lambda g: jnp.tile(g.reshape(1, HEAD_DIM), (1, n_heads))
    vec = lambda n: pl.BlockSpec((1, n), const)
    grid_spec = pltpu.PrefetchScalarGridSpec(
        num_scalar_prefetch=3,
        grid=(nt,),
        in_specs=[xp_spec, xs_spec,
                  halo(xp.shape[0], prompt_tile, -1), halo(xp.shape[0], prompt_tile, +1),
                  halo(xs.shape[0], sample_tile, -1), halo(xs.shape[0], sample_tile, +1),
                  vec(D_MODEL),
                  pl.BlockSpec((1, 1, D_MODEL), lambda i, mr, *_: (mr[i] * 6 + 0, 0, 0)),
                  pl.BlockSpec((1, 1, D_MODEL), lambda i, mr, *_: (mr[i] * 6 + 1, 0, 0)),
                  resident((D_MODEL, IN_WIDTH)), vec(ATTN_WIDTH), vec(ATTN_WIDTH), resident((ATTN_WIDTH, ATTN_WIDTH)),
                  vec(RWKV_IN), vec(RWKV_IN), vec(2 * c), resident((128, 2 * c)), vec(2 * c), resident((128, 2 * c)),
                  resident((128, c)), vec(c), vec(c), vec(c)],
        out_specs=[pl.BlockSpec((ROW_TILE, ATTN_WIDTH), row)] * 3 + [pl.BlockSpec((ROW_TILE, c), row)] * 5
                  + [pl.BlockSpec((2, ROW_TILE, c), lambda i, *_: (0, i, 0))] * 3,
    )
    tok = lambda dt: jax.ShapeDtypeStruct((ntok, c), dt)
    tok2 = lambda dt: jax.ShapeDtypeStruct((2, ntok, c), dt)
    return pl.pallas_call(
        functools.partial(_inproj_kernel, n_prompt_tiles),
        grid_spec=grid_spec,
        out_shape=[jax.ShapeDtypeStruct((ntok, ATTN_WIDTH), F32)] * 3
                  + [tok(BF16), tok(BF16), tok(BF16), tok(F32), tok(F32), tok2(F32), tok2(BF16), tok2(BF16)],
        compiler_params=pltpu.CompilerParams(dimension_semantics=("arbitrary",), vmem_limit_bytes=VMEM_LIMIT),
        name="inproj",
    )(tabs["mrow_tile"], tabs["first"], tabs["last"], xp, xs, xp, xp, xs, xs,
      g_norm1.reshape(1, -1), mod3, mod3, w_in_bf, per_head(q_norm_g), per_head(k_norm_g), _head_ones(),
      mu_prev.reshape(1, -1), mu_next.reshape(1, -1), w0.reshape(1, -1), _block_diag2(w_up).astype(BF16),
      a0.reshape(1, -1), _block_diag2(a_up).astype(BF16), g_up.astype(BF16),
      k_k.reshape(1, -1), k_a.reshape(1, -1), r_k.reshape(1, -1))


def _lane_is_head0(shape):
    return lax.broadcasted_iota(jnp.int32, shape, len(shape) - 1) < HEAD_DIM


def _attn_kernel(hasprev_ref, hasnext_ref,
                 q_ref, kp_ref, kc_ref, kx_ref, vp_ref, vc_ref, vx_ref, beta_ref, bias_ref,
                 o_ref,
                 qd, kd, vd, tq, tk, tv, acc_o, acc_m):
    i = pl.program_id(0)
    has_prev = hasprev_ref[i]
    has_next = hasnext_ref[i]
    rows = 256

    m0_o = _lane_is_head0((HALF, 128))

    for di, dil in enumerate(DILATIONS):
        lq = SEQ_BLOCK // dil
        lk = lq + 2 * HALF
        nbq = lq // HALF
        pbase = HALO - HALF * dil

        if dil == 1:
            def cast_rows(src_ref, src_off, dst_ref, dst_off, nrows, chunk):
                def body(t, carry):
                    r0 = pl.multiple_of(t * chunk, chunk)
                    dst_ref[pl.ds(dst_off + r0, chunk), :] = src_ref[pl.ds(src_off + r0, chunk), :].astype(BF16)
                    return carry
                lax.fori_loop(0, nrows // chunk, body, 0)
            cast_rows(q_ref, 0, qd, 0, lq, rows)
            for prev_ref, cur_ref, next_ref, dst in ((kp_ref, kc_ref, kx_ref, kd), (vp_ref, vc_ref, vx_ref, vd)):
                cast_rows(prev_ref, pbase, dst, 0, HALF, HALF)
                cast_rows(cur_ref, 0, dst, HALF, lq, rows)
                cast_rows(next_ref, 0, dst, HALF + lq, HALF, HALF)
        elif dil == 16:
            assert pbase == 0 and lk * dil == SEQ_BLOCK + 2 * HALO
            wq, wk = SEQ_BLOCK // 4, (SEQ_BLOCK + 2 * HALO) // 4

            def split4(r0, carry, wq=wq, wk=wk):
                tq[pl.ds(pl.multiple_of(r0 * wq, HALF), wq), :] = q_ref[pl.ds(r0, wq, stride=4), :]
                base = pl.multiple_of(r0 * wk, HALF)
                for prev_ref, cur_ref, next_ref, dst in ((kp_ref, kc_ref, kx_ref, tk), (vp_ref, vc_ref, vx_ref, tv)):
                    dst[pl.ds(base, HALO // 4), :] = prev_ref[pl.ds(r0, HALO // 4, stride=4), :]
                    dst[pl.ds(base + HALO // 4, wq), :] = cur_ref[pl.ds(r0, wq, stride=4), :]
                    dst[pl.ds(base + HALO // 4 + wq, HALO // 4), :] = next_ref[pl.ds(r0, HALO // 4, stride=4), :]
                return carry
            lax.fori_loop(0, 4, split4, 0)

            def regroup16(r, carry, lq=lq, lk=lk, wq=wq, wk=wk):
                r0 = r & 3
                r1 = r >> 2
                qd[pl.ds(pl.multiple_of(r * lq, HALF), lq), :] = tq[pl.ds(r0 * wq + r1, lq, stride=4), :].astype(BF16)
                ko = pl.multiple_of(r * lk, HALF)
                kd[pl.ds(ko, lk), :] = tk[pl.ds(r0 * wk + r1, lk, stride=4), :].astype(BF16)
                vd[pl.ds(ko, lk), :] = tv[pl.ds(r0 * wk + r1, lk, stride=4), :].astype(BF16)
                return carry
            lax.fori_loop(0, dil, regroup16, 0)
        else:
            def regroup(r, carry, dil=dil, lq=lq, lk=lk, pbase=pbase):
                qd[pl.ds(pl.multiple_of(r * lq, HALF), lq), :] = q_ref[pl.ds(r, lq, stride=dil), :].astype(BF16)
                ko = pl.multiple_of(r * lk, HALF)
                for prev_ref, cur_ref, next_ref, dst in ((kp_ref, kc_ref, kx_ref, kd), (vp_ref, vc_ref, vx_ref, vd)):
                    dst[pl.ds(ko, HALF), :] = prev_ref[pl.ds(pbase + r, HALF, stride=dil), :].astype(BF16)
                    dst[pl.ds(ko + HALF, lq), :] = cur_ref[pl.ds(r, lq, stride=dil), :].astype(BF16)
                    dst[pl.ds(ko + HALF + lq, HALF), :] = next_ref[pl.ds(r, HALF, stride=dil), :].astype(BF16)
                return carry
            lax.fori_loop(0, dil, regroup, 0)

        shift = nbq.bit_length() - 1

        def bands(it, carry, di=di, dil=dil, nbq=nbq, shift=shift):
            work = []
            for u in range(BANDS_PER_STEP):
                idx = it * BANDS_PER_STEP + u
                r = idx >> shift
                b = idx & (nbq - 1)
                qo = pl.multiple_of(idx * HALF, HALF)
                ko = pl.multiple_of(idx * HALF + r * (2 * HALF), HALF)
                variant = jnp.where(jnp.logical_and(b == 0, has_prev == 0), 1,
                                    jnp.where(jnp.logical_and(b == nbq - 1, has_next == 0), 2, 0))
                work.append((r + (dil * HALF) * b, di * 3 + variant, qd[pl.ds(qo, HALF), :],
                             kd[pl.ds(ko, 3 * HALF), :], vd[pl.ds(ko, 3 * HALF), :]))
            s, m, l, o = [], [], [], []
            for w in work:
                si = _dot_nt(_stack_heads(w[2]), w[3]) + bias_ref[w[1]]
                s.append(si)
                m.append(jnp.max(si, axis=-1, keepdims=True))
            for w, si, mi in zip(work, s, m):
                pi = jnp.exp2(si - mi)
                l.append(jnp.sum(pi, axis=-1, keepdims=True))
                o.append(_dot(pi.astype(BF16), w[4]))
            for w, oi, mi, li in zip(work, o, m, l):
                if dil == 1:
                    dst = pl.ds(pl.multiple_of(w[0], HALF), HALF)
                else:
                    dst = pl.ds(w[0], HALF, stride=dil)
                on = oi * (1.0 / li)
                lse = mi + jnp.log2(li)
                acc_o[di][dst, :] = jnp.where(m0_o, on[0:HALF], on[HALF:2 * HALF])
                acc_m[di][dst, :] = jnp.where(m0_o, lse[0:HALF], lse[HALF:2 * HALF])
            return carry

        lax.fori_loop(0, SEQ_BLOCK // HALF // BANDS_PER_STEP, bands, 0)

    beta = beta_ref[...]

    def merge(t, carry):
        r0 = pl.multiple_of(t * rows, rows)
        sl = pl.ds(r0, rows)
        ms = [acc_m[d][sl, :] for d in range(3)]
        mx = jnp.maximum(jnp.maximum(ms[0], ms[1]), ms[2])
        num = jnp.zeros((rows, 128), F32)
        den = jnp.zeros((rows, 128), F32)
        for d in range(3):
            e = jnp.exp2(ms[d] - mx)
            num = num + e * acc_o[d][sl, :]
            den = den + e
        o_ref[sl, :] = ((num / den) * beta).astype(o_ref.dtype)
        return carry

    lax.fori_loop(0, SEQ_BLOCK // rows, merge, 0)


def _attn_bias_table():
    qi = np.arange(HALF)[:, None]
    kj = np.arange(3 * HALF)[None, :]
    rel = np.abs(kj - HALF - qi).astype(np.float32)
    n_heads = ATTN_WIDTH // HEAD_DIM
    slopes = 2.0 ** (-8.0 * (np.arange(n_heads, dtype=np.float32) + 1.0) / n_heads)
    keep = [rel <= HALF, (rel <= HALF) & (kj >= HALF), (rel <= HALF) & (kj < 2 * HALF)]
    tab = np.empty((N_PAIRS, 3, 3, 2, HALF, 3 * HALF), np.float32)
    for h in range(n_heads):
        for di, dil in enumerate(DILATIONS):
            for var in range(3):
                tab[h // 2, di, var, h % 2] = np.where(keep[var], -(slopes[h] * (dil * rel)) * LOG2_E, NEG_INF)
    return jnp.asarray(tab.reshape(N_PAIRS * 9, 2 * HALF, 3 * HALF))


def _attention(q, k, v, has_prev, has_next, attn_beta):
    ntok = q.shape[0]
    nb = ntok // SEQ_BLOCK
    hb = SEQ_BLOCK // HALO
    cur = lambda i, j, hp, hn: (i, j)
    prev = lambda i, j, hp, hn: (jnp.maximum(i * hb - 1, 0), j)
    nxt = lambda i, j, hp, hn: (jnp.minimum((i + 1) * hb, nb * hb - 1), j)
    blk = pl.BlockSpec((SEQ_BLOCK, 128), cur)
    halo_p = pl.BlockSpec((HALO, 128), prev)
    halo_n = pl.BlockSpec((HALO, 128), nxt)
    win = SEQ_BLOCK + 2 * HALO
    grid_spec = pltpu.PrefetchScalarGridSpec(
        num_scalar_prefetch=2,
        grid=(nb, N_PAIRS),
        in_specs=[blk, halo_p, blk, halo_n, halo_p, blk, halo_n,
                  pl.BlockSpec((1, 128), lambda i, j, hp, hn: (0, j)),
                  pl.BlockSpec((9, 2 * HALF, 3 * HALF), lambda i, j, hp, hn: (j, 0, 0))],
        out_specs=blk,
        scratch_shapes=[pltpu.VMEM((SEQ_BLOCK, 128), BF16), pltpu.VMEM((win, 128), BF16), pltpu.VMEM((win, 128), BF16),
                        pltpu.VMEM((SEQ_BLOCK, 128), F32), pltpu.VMEM((win, 128), F32), pltpu.VMEM((win, 128), F32),
                        [pltpu.VMEM((SEQ_BLOCK, 128), F32)] * 3,
                        [pltpu.VMEM((SEQ_BLOCK, 128), F32)] * 3],
    )
    return pl.pallas_call(
        _attn_kernel,
        grid_spec=grid_spec,
        out_shape=jax.ShapeDtypeStruct((ntok, ATTN_WIDTH), BF16),
        compiler_params=pltpu.CompilerParams(dimension_semantics=("arbitrary", "arbitrary"),
                                             vmem_limit_bytes=VMEM_LIMIT),
        name="attn",
    )(has_prev, has_next, q, k, k, k, v, v, v, attn_beta.reshape(1, -1), _attn_bias_table())


def _block_diag2(m):
    z = jnp.zeros_like(m[0])
    return jnp.concatenate([jnp.concatenate([m[0], z], axis=1), jnp.concatenate([z, m[1]], axis=1)], axis=0)


def _head_ones():
    idx = np.arange(RWKV_WIDTH) // HEAD_DIM
    return jnp.asarray((idx[:, None] == idx[None, :]).astype(np.float32), dtype=BF16)


def _stack_heads(x):
    m0 = _lane_is_head0(x.shape)
    z = jnp.zeros_like(x)
    return jnp.concatenate([jnp.where(m0, x, z), jnp.where(m0, z, x)], axis=0)


def _stack_heads_bf16(x):
    return _stack_heads(x.astype(BF16))


def _scan_masks(rev):
    c = CHUNK
    rowi = lax.broadcasted_iota(jnp.int32, (c, 128), 0)
    colj = lax.broadcasted_iota(jnp.int32, (c, 128), 1) & (c - 1)
    ti = lax.broadcasted_iota(jnp.int32, (c, c), 0)
    tj = lax.broadcasted_iota(jnp.int32, (c, c), 1)
    if rev:
        strict, incl, tri = colj > rowi, colj >= rowi, tj >= ti
    else:
        strict, incl, tri = colj < rowi, colj <= rowi, tj <= ti
    return dict(strict=strict, incl=incl, tri=tri.astype(BF16), eye2=(colj == rowi).astype(F32))


def _scan_pre(ins, masks):
    c = CHUNK
    idx = range(len(ins))
    lw = [a[0] for a in ins]
    v = [a[3] for a in ins]

    cl = []
    for i in idx:
        hi, mid, lo = _split3(lw[i])
        cl3 = _dot(masks[i]["tri"], jnp.concatenate([hi, mid, lo], axis=1))
        cl.append(cl3[:, 0:128] + cl3[:, 128:256] + cl3[:, 256:384])
    yield
    g_tot = [jnp.exp(cl[i][0:1, :] if masks[i]["rev"] else cl[i][c - 1:c, :]) for i in idx]
    g_inv = [jnp.exp(-cl[i]) for i in idx]
    rt = [ins[i][1] * jnp.exp(cl[i]) for i in idx]
    bt = [ins[i][2] * jnp.exp(cl[i] - lw[i]) for i in idx]
    at = [-(ins[i][5] * g_inv[i]) for i in idx]
    kt = [ins[i][4] * g_inv[i] for i in idx]
    br = [jnp.concatenate([bt[i], rt[i]], axis=0).astype(BF16) for i in idx]
    lm = [_dot_nt(br[i], jnp.concatenate([_stack_heads_bf16(at[i]), _stack_heads_bf16(kt[i])], axis=0))
          for i in idx]
    akg = [jnp.concatenate([at[i] * g_tot[i], kt[i] * g_tot[i]], axis=0).astype(BF16) for i in idx]
    yield
    la = [jnp.where(masks[i]["strict"], lm[i][0:c, 0:128], 0.0) for i in idx]
    lk = [jnp.where(masks[i]["strict"], lm[i][0:c, 128:256], 0.0).astype(BF16) for i in idx]
    mam = [jnp.concatenate([jnp.where(masks[i]["incl"], lm[i][c:2 * c, 0:128], 0.0),
                            jnp.where(masks[i]["incl"], lm[i][c:2 * c, 128:256], 0.0)], axis=1).astype(BF16)
           for i in idx]

    sv = [_stack_heads_bf16(v[i]) for i in idx]
    lkv = [_dot(lk[i], sv[i]) for i in idx]

    t = [masks[i]["eye2"] + la[i] for i in idx]
    pw = [_dot(la[i].astype(BF16), _stack_heads_bf16(la[i])) for i in idx]
    yield
    for step in range(5):
        if step < 4:
            out = [_dot(pw[i].astype(BF16),
                        jnp.concatenate([_stack_heads_bf16(pw[i]), _stack_heads_bf16(t[i])], axis=1))
                   for i in idx]
            t = [t[i] + out[i][:, 128:256] for i in idx]
            pw = [out[i][:, 0:128] for i in idx]
        else:
            t = [t[i] + _dot(pw[i].astype(BF16), _stack_heads_bf16(t[i])) for i in idx]
        yield
    return [dict(br=br[i], lkv=lkv[i], mam=mam[i], sv=sv[i], t=t[i].astype(BF16), v=v[i].astype(BF16),
                 akg=akg[i], g_tot=g_tot[i]) for i in idx]


def _scan_seq(pre, zs, n_chunks):
    c = CHUNK
    per = len(zs)
    idx = range(per)
    zr = lax.broadcasted_iota(jnp.int32, (128, 128), 0) < HEAD_DIM
    zc = lax.broadcasted_iota(jnp.int32, (128, 128), 1) < HEAD_DIM
    same_head = zr == zc
    ys = []
    for k in range(n_chunks):
        p = pre[k * per:(k + 1) * per]
        brz = [_dot_nt(p[i]["br"], zs[i].astype(BF16)) for i in idx]
        yield
        u = [_dot(p[i]["t"], _stack_heads_bf16(brz[i][0:c] + p[i]["lkv"])) for i in idx]
        yield
        ys += [brz[i][c:2 * c] + _dot(p[i]["mam"], jnp.concatenate([_stack_heads_bf16(u[i]), p[i]["sv"]], axis=0))
               for i in idx]
        zs = [zs[i] * p[i]["g_tot"]
              + jnp.where(same_head, _dot_tn(jnp.concatenate([u[i].astype(BF16), p[i]["v"]], axis=0), p[i]["akg"]), 0.0)
              for i in idx]
        yield
    return ys, zs


def _run(gen):
    try:
        while True:
            next(gen)
    except StopIteration as done:
        return done.value


def _interleave(gen_a, gen_b):
    live = {"a": gen_a, "b": gen_b}
    result = {}
    while live:
        for name in list(live):
            try:
                next(live[name])
            except StopIteration as done:
                result[name] = done.value
                del live[name]
    return result["a"], result["b"]


def _scan_kernel(bmap_ref, reset_ref,
                 rf, kkf, vf, lwf, kdf, bdf, rb, kkb, vb, lwb, kdb, bdb,
                 yf_ref, yb_ref, zf, zb):
    del bmap_ref
    g = pl.program_id(0)

    @pl.when(reset_ref[g] == 1)
    def _():
        zf[...] = jnp.zeros_like(zf)
        zb[...] = jnp.zeros_like(zb)

    nchunk = SCAN_BLOCK // CHUNK
    n_groups = nchunk // CHUNKS_PER_GROUP
    mask_f = dict(_scan_masks(False), rev=False)
    mask_b = dict(_scan_masks(True), rev=True)
    dirs = ((mask_f, (lwf, rf, kkf, vf, kdf, bdf), yf_ref, zf),
            (mask_b, (lwb, rb, kkb, vb, kdb, bdb), yb_ref, zb))

    def group_items(gi):
        ins, masks, outs = [], [], []
        for k in range(CHUNKS_PER_GROUP):
            step = gi * CHUNKS_PER_GROUP + k
            for mask, refs, y_ref, _ in dirs:
                cc = (nchunk - 1 - step) if mask["rev"] else step
                rows = slice(cc * CHUNK, (cc + 1) * CHUNK)
                for pr in range(N_PAIRS):
                    lanes = slice(128 * pr, 128 * (pr + 1))
                    ins.append([ref[rows, lanes] for ref in refs])
                    masks.append(mask)
                    outs.append((y_ref, rows, lanes))
        return ins, masks, outs

    zs = [z_ref[pr] for _, _, _, z_ref in dirs for pr in range(N_PAIRS)]
    ins, masks, outs = group_items(0)
    pre = _run(_scan_pre(ins, masks))
    for gi in range(n_groups):
        seq = _scan_seq(pre, zs, CHUNKS_PER_GROUP)
        if gi + 1 < n_groups:
            ins, masks, next_outs = group_items(gi + 1)
            pre, (ys, zs) = _interleave(_scan_pre(ins, masks), seq)
        else:
            ys, zs = _run(seq)
        for (y_ref, rows, lanes), y in zip(outs, ys):
            y_ref[rows, lanes] = y
        if gi + 1 < n_groups:
            outs = next_outs
    state_refs = [(z_ref, pr) for _, _, _, z_ref in dirs for pr in range(N_PAIRS)]
    for (z_ref, pr), z in zip(state_refs, zs):
        z_ref[pr] = z


def _scan(r, kk, v, lw, kd, bd, bmap_bwd, reset):
    ntok = r.shape[0]
    c = RWKV_WIDTH
    ns = ntok // SCAN_BLOCK
    fwd = lambda g, bm, rs: (g, 0)
    bwd = lambda g, bm, rs: (bm[g], 0)
    fwd2 = lambda g, bm, rs: (0, g, 0)
    bwd2 = lambda g, bm, rs: (1, bm[g], 0)
    tokf = pl.BlockSpec((SCAN_BLOCK, c), fwd)
    tokb = pl.BlockSpec((SCAN_BLOCK, c), bwd)
    dirf = pl.BlockSpec((None, SCAN_BLOCK, c), fwd2)
    dirb = pl.BlockSpec((None, SCAN_BLOCK, c), bwd2)
    grid_spec = pltpu.PrefetchScalarGridSpec(
        num_scalar_prefetch=2,
        grid=(ns,),
        in_specs=[tokf, tokf, tokf, dirf, dirf, dirf, tokb, tokb, tokb, dirb, dirb, dirb],
        out_specs=[tokf, tokb],
        scratch_shapes=[pltpu.VMEM((N_PAIRS, 128, 128), F32), pltpu.VMEM((N_PAIRS, 128, 128), F32)],
    )
    tok = jax.ShapeDtypeStruct((ntok, c), F32)
    return pl.pallas_call(
        _scan_kernel,
        grid_spec=grid_spec,
        out_shape=[tok, tok],
        compiler_params=pltpu.CompilerParams(dimension_semantics=("arbitrary",), vmem_limit_bytes=VMEM_LIMIT),
        name="rwkv_scan",
    )(bmap_bwd, reset, r, kk, v, lw, kd, bd, r, kk, v, lw, kd, bd)


def _post_ffn_kernel(n_prompt_tiles, mrow_ref, xp_ref, xs_ref, yf_ref, yb_ref, bonus_ref, g_ref, attn_ref,
                     gt1_ref, sh2_ref, sc2_ref, gt2_ref, g2_ref, lnw_ref, lnb_ref, ones_ref, wo_ref, w1_ref, w2_ref,
                     op_ref, os_ref):
    del mrow_ref
    is_prompt = pl.program_id(0) < n_prompt_tiles
    x = jnp.where(is_prompt, xp_ref[...], xs_ref[...])

    y = yf_ref[...] + yb_ref[...]
    ones_bd = ones_ref[...]
    y1, y2, _ = _split3(y)
    mu = (_dot(y1, ones_bd) + _dot(y2, ones_bd)) * (1.0 / HEAD_DIM)
    d = y - mu
    var = _dot((d * d).astype(BF16), ones_bd) * (1.0 / HEAD_DIM)
    yn = d * lax.rsqrt(var + LN_X_EPS) * lnw_ref[...] + lnb_ref[...]
    rw = ((yn + bonus_ref[...]) * g_ref[...]).astype(BF16)

    a = ATTN_WIDTH
    mix = _dot(attn_ref[...], wo_ref[0:a, :]) + _dot(rw, wo_ref[a:, :])
    x1 = x + gt1_ref[0] * mix
    h = _modulated_rmsnorm(x1, g2_ref[...], sc2_ref[0], sh2_ref[0]).astype(BF16)

    acc = jnp.zeros((h.shape[0], D_MODEL), F32)
    for j in range(D_FF // FF_STEP):
        cols = slice(j * FF_STEP, (j + 1) * FF_STEP)
        act = jnp.maximum(_dot(h, w1_ref[:, cols]), 0.0)
        acc = acc + _dot((act * act).astype(BF16), w2_ref[cols, :])
    out = x1 + gt2_ref[0] * acc

    @pl.when(is_prompt)
    def _():
        op_ref[...] = out

    @pl.when(jnp.logical_not(is_prompt))
    def _():
        os_ref[...] = out


def _post_ffn(xp, xs, yf, yb, bonus, g, attn, mod3, mrow_tile, g_norm2, ln_x_w, ln_x_b, w_out_bf, w1_bf, w2_bf):
    n_prompt_tiles = xp.shape[0] // ROW_TILE
    nt = (xp.shape[0] + xs.shape[0]) // ROW_TILE
    c = RWKV_WIDTH
    row = lambda i, mr: (i, 0)
    const = lambda i, mr: (0, 0)
    resident = lambda shape: pl.BlockSpec(shape, const, pipeline_mode=pl.Buffered(1))
    modspec = lambda j: pl.BlockSpec((1, 1, D_MODEL), lambda i, mr: (mr[i] * 6 + j, 0, 0))
    xp_spec, xs_spec = _two_group_specs(n_prompt_tiles)
    half = pl.BlockSpec((ROW_TILE, c), row)
    grid_spec = pltpu.PrefetchScalarGridSpec(
        num_scalar_prefetch=1,
        grid=(nt,),
        in_specs=[xp_spec, xs_spec, half, half, half, half, half,
                  modspec(2), modspec(3), modspec(4), modspec(5),
                  pl.BlockSpec((1, D_MODEL), const), pl.BlockSpec((1, c), const), pl.BlockSpec((1, c), const),
                  resident((c, c)), resident((D_MODEL, D_MODEL)), resident((D_MODEL, D_FF)),
                  resident((D_FF, D_MODEL))],
        out_specs=[xp_spec, xs_spec],
    )
    return pl.pallas_call(
        functools.partial(_post_ffn_kernel, n_prompt_tiles),
        grid_spec=grid_spec,
        out_shape=[jax.ShapeDtypeStruct(xp.shape, F32), jax.ShapeDtypeStruct(xs.shape, F32)],
        compiler_params=pltpu.CompilerParams(dimension_semantics=("arbitrary",), vmem_limit_bytes=VMEM_LIMIT),
        name="post_ffn",
    )(mrow_tile, xp, xs, yf, yb, bonus, g, attn, mod3, mod3, mod3, mod3, g_norm2.reshape(1, -1),
      ln_x_w.reshape(1, -1), ln_x_b.reshape(1, -1), _head_ones(), w_out_bf, w1_bf, w2_bf)


def _sequence_tables(seq_blocks):
    mod_row, has_prev, has_next = [], [], []
    for s, n in enumerate(seq_blocks):
        for b in range(n):
            mod_row.append(s)
            has_prev.append(int(b > 0))
            has_next.append(int(b < n - 1))
    per = SEQ_BLOCK // SCAN_BLOCK
    bmap_bwd, reset = [], []
    start = 0
    for n in seq_blocks:
        ns = n * per
        for t in range(ns):
            bmap_bwd.append(start + ns - 1 - t)
            reset.append(int(t == 0))
        start += ns
    i32 = lambda a: jnp.asarray(np.asarray(a, np.int32))
    rep = lambda a, k: np.repeat(np.asarray(a, np.int32), k)
    row_tiles = SEQ_BLOCK // ROW_TILE
    first = np.zeros(len(mod_row) * row_tiles, np.int32)
    last = np.zeros(len(mod_row) * row_tiles, np.int32)
    for b in range(len(mod_row)):
        if not has_prev[b]:
            first[b * row_tiles] = 1
        if not has_next[b]:
            last[(b + 1) * row_tiles - 1] = 1
    return dict(mrow_tile=i32(rep(mod_row, row_tiles)), has_prev=i32(has_prev), has_next=i32(has_next),
                bmap_bwd=i32(bmap_bwd), reset=i32(reset), first=i32(first), last=i32(last))


def _layer(xp, xs, c_all, tabs, w_ada, b_ada, g_norm1, g_norm2, w_in, q_norm_g, k_norm_g, attn_beta,
           mu_prev, mu_next, w0, w_up, a0, a_up, g_up, k_k, k_a, r_k, ln_x_w, ln_x_b, w_out, w_ff1, w_ff2):
    mod = _ada(c_all, w_ada, b_ada)
    mod3 = mod.reshape(-1, 1, D_MODEL)
    q, k, v, r, kk, vv, g, bonus, lw, kd, bd = _inproj(
        xp, xs, mod3, tabs, g_norm1, w_in.astype(BF16), q_norm_g, k_norm_g,
        mu_prev, mu_next, w0, w_up, a0, a_up, g_up, k_k, k_a, r_k)
    attn = _attention(q, k, v, tabs["has_prev"], tabs["has_next"], attn_beta)
    yf, yb = _scan(r, kk, vv, lw, kd, bd, tabs["bmap_bwd"], tabs["reset"])
    return _post_ffn(xp, xs, yf, yb, bonus, g, attn, mod3, tabs["mrow_tile"], g_norm2, ln_x_w, ln_x_b,
                     w_out.astype(BF16), w_ff1.astype(BF16), w_ff2.astype(BF16))


def kernel(x_prompt, x_sample, c_prompt, c_sample, w_ada, b_ada, g_norm1, g_norm2, w_in, q_norm_g, k_norm_g, attn_beta, mu_prev, mu_next, w0, w_up, a0, a_up, g_up, k_k, k_a, r_k, ln_x_w, ln_x_b, w_out, w_ff1, w_ff2):
    bp, sp, d = x_prompt.shape
    bs, ss, _ = x_sample.shape
    assert d == D_MODEL and sp % SEQ_BLOCK == 0 and ss % SEQ_BLOCK == 0
    seq_blocks = [sp // SEQ_BLOCK] * bp + [ss // SEQ_BLOCK] * bs
    tabs = _sequence_tables(seq_blocks)
    n_seq = bp + bs
    pad = (-n_seq) % 8
    c_all = jnp.concatenate([c_prompt, c_sample, jnp.zeros((pad, d), F32)], axis=0)
    yp = x_prompt.reshape(bp * sp, d)
    ys = x_sample.reshape(bs * ss, d)
    for i in range(w_ada.shape[0]):
        yp, ys = _layer(yp, ys, c_all, tabs, w_ada[i], b_ada[i], g_norm1[i], g_norm2[i], w_in[i], q_norm_g[i],
                        k_norm_g[i], attn_beta[i], mu_prev[i], mu_next[i], w0[i], w_up[i], a0[i], a_up[i], g_up[i],
                        k_k[i], k_a[i], r_k[i], ln_x_w[i], ln_x_b[i], w_out[i], w_ff1[i], w_ff2[i])
    return (yp.reshape(bp, sp, d), ys.reshape(bs, ss, d))
```

```python
import functools

import numpy as np
import jax
import jax.numpy as jnp
from jax import lax
from jax.experimental import pallas as pl
from jax.experimental.pallas import tpu as pltpu

F32 = jnp.float32
BF16 = jnp.bfloat16

D_MODEL = 1024
HEAD_DIM = 64
ATTN_WIDTH = 512
RWKV_WIDTH = 512
N_PAIRS = ATTN_WIDTH // 128
RWKV_IN = 1920
IN_WIDTH = 3 * ATTN_WIDTH + RWKV_IN
D_FF = 4096
DILATIONS = (1, 4, 16)
HALF = 64
NORM_EPS = 1e-6
LN_X_EPS = 64e-5
NEG_INF = -1e30
LOG2_E = 1.4426950408889634

SEQ_BLOCK = 2048
HALO = 1024
SCAN_BLOCK = 512
CHUNK = 64
CHUNKS_PER_GROUP = 2
ROW_TILE = 512
BANDS_PER_STEP = 32
FF_STEP = 1024

VMEM_LIMIT = 56 * 1024 * 1024


def _dot(a, b):
    return jnp.dot(a, b, preferred_element_type=F32)


def _dot_nt(a, b):
    return lax.dot_general(a, b, (((1,), (1,)), ((), ())), preferred_element_type=F32)


def _dot_tn(a, b):
    return lax.dot_general(a, b, (((0,), (0,)), ((), ())), preferred_element_type=F32)


def _split3(x):
    hi = x.astype(BF16)
    r1 = x - hi.astype(F32)
    mid = r1.astype(BF16)
    lo = (r1 - mid.astype(F32)).astype(BF16)
    return hi, mid, lo


def _sigmoid(x):
    return 1.0 / (1.0 + jnp.exp(-x))


def _ada_kernel(c_ref, w_ref, b_ref, o_ref):
    c = c_ref[...]
    s = c * _sigmoid(c)
    s1, s2, _ = _split3(s)
    w = w_ref[...]
    w1, w2, _ = _split3(w)
    o_ref[...] = _dot(s1, w1) + (_dot(s1, w2) + _dot(s2, w1)) + b_ref[...]


def _ada(c_all, w_ada, b_ada):
    n = c_all.shape[0]
    nt = w_ada.shape[1] // D_MODEL
    return pl.pallas_call(
        _ada_kernel,
        grid=(nt,),
        in_specs=[pl.BlockSpec((n, D_MODEL), lambda j: (0, 0)),
                  pl.BlockSpec((D_MODEL, D_MODEL), lambda j: (0, j)),
                  pl.BlockSpec((1, D_MODEL), lambda j: (0, j))],
        out_specs=pl.BlockSpec((n, D_MODEL), lambda j: (0, j)),
        out_shape=jax.ShapeDtypeStruct((n, w_ada.shape[1]), F32),
        compiler_params=pltpu.CompilerParams(dimension_semantics=("arbitrary",), vmem_limit_bytes=VMEM_LIMIT),
        name="ada",
    )(c_all, w_ada, b_ada.reshape(1, -1))


def _modulated_rmsnorm(x, g, scale, shift):
    ms = jnp.mean(x * x, axis=-1, keepdims=True)
    return (x * lax.rsqrt(ms + NORM_EPS) * g) * (1.0 + scale) + shift


def _two_group_specs(n_prompt_tiles):
    def prompt(i, *_):
        return (jnp.minimum(i, n_prompt_tiles - 1), 0)

    def sample(i, *_):
        return (jnp.maximum(i - n_prompt_tiles, 0), 0)
    return pl.BlockSpec((ROW_TILE, D_MODEL), prompt), pl.BlockSpec((ROW_TILE, D_MODEL), sample)


def _head_rms_scale(z, ones_bd):
    ssq = _dot((z * z).astype(BF16), ones_bd)
    return lax.rsqrt(ssq * (1.0 / HEAD_DIM) + NORM_EPS)


def _inproj_kernel(n_prompt_tiles, mrow_ref, first_ref, last_ref,
                   xp_ref, xs_ref, xpp_ref, xpn_ref, xsp_ref, xsn_ref,
                   g_ref, sh_ref, sc_ref, w_ref, qg_ref, kg_ref, ones_ref,
                   mup_ref, mun_ref, w0_ref, wup_ref, a0_ref, aup_ref, gup_ref, kk_ref, ka_ref, rk_ref,
                   q_ref, k_ref, v_ref,
                   r_out, kk_out, v_out, g_out, bonus_out, lw_out, kd_out, bd_out):
    del mrow_ref
    i = pl.program_id(0)
    is_prompt = i < n_prompt_tiles
    g1, sc, sh = g_ref[...], sc_ref[0], sh_ref[0]
    keep_prev = jnp.where(first_ref[i] == 0, 1.0, 0.0)
    keep_next = jnp.where(last_ref[i] == 0, 1.0, 0.0)
    h = _modulated_rmsnorm(jnp.where(is_prompt, xp_ref[...], xs_ref[...]), g1, sc, sh)
    h_prev = _modulated_rmsnorm(jnp.where(is_prompt, xpp_ref[...], xsp_ref[...]), g1, sc, sh) * keep_prev
    h_next = _modulated_rmsnorm(jnp.where(is_prompt, xpn_ref[...], xsn_ref[...]), g1, sc, sh) * keep_next

    a = ATTN_WIDTH
    hb = h.astype(BF16)
    ones_bd = ones_ref[...]

    tm = h.shape[0]
    hext = jnp.concatenate([h_prev, h, h_next], axis=0).astype(BF16)
    next_amt = hext.shape[0] - 1

    def shifted(lo, hi):
        p = _dot(hext, w_ref[:, 3 * a + lo:3 * a + hi])
        prev = pltpu.roll(p, 1, 0)
        nxt = pltpu.roll(p, next_amt, 0)
        ps = p + mup_ref[:, lo:hi] * (prev - p) + mun_ref[:, lo:hi] * (nxt - p)
        return ps[8:8 + tm]

    c = RWKV_WIDTH
    low = shifted(3 * c, 3 * c + 384)
    wd = low[:, 0:128]
    ad = low[:, 128:256]
    gd = low[:, 256:384]
    k = shifted(c, 2 * c)

    w_raw = w0_ref[...] + _dot(jnp.tanh(wd).astype(BF16), wup_ref[...])
    lw = (-float(np.exp(-0.5))) * _sigmoid(w_raw)
    lw_out[0] = lw[:, 0:c]
    lw_out[1] = lw[:, c:2 * c]
    r = shifted(0, c)
    a_rate = _sigmoid(a0_ref[...] + _dot(ad.astype(BF16), aup_ref[...]))
    g_out[...] = _dot(_sigmoid(gd).astype(BF16), gup_ref[...])
    v = shifted(2 * c, 3 * c)

    kkv = k * kk_ref[...]
    ssq = _dot((kkv * kkv).astype(BF16), ones_bd)
    kkv = kkv * lax.rsqrt(jnp.maximum(ssq, 1e-24))
    ka = ka_ref[...]
    kd0 = k * (1.0 + (a_rate[:, 0:c] - 1.0) * ka)
    kd1 = k * (1.0 + (a_rate[:, c:2 * c] - 1.0) * ka)
    hsum = _dot((r * (kd0 + kd1) * rk_ref[...]).astype(BF16), ones_bd)

    q_att = _dot(hb, w_ref[:, 0:a])

    r_out[...] = r.astype(BF16)
    kk_out[...] = kkv.astype(BF16)
    v_out[...] = v.astype(BF16)
    bonus_out[...] = hsum * v
    kd_out[0] = kd0.astype(BF16)
    kd_out[1] = kd1.astype(BF16)
    bd_out[0] = (kkv * a_rate[:, 0:c]).astype(BF16)
    bd_out[1] = (kkv * a_rate[:, c:2 * c]).astype(BF16)

    k_att = _dot(hb, w_ref[:, a:2 * a])
    q_ref[...] = q_att * _head_rms_scale(q_att, ones_bd) * (qg_ref[...] * (HEAD_DIM ** -0.5 * LOG2_E))
    v_ref[...] = _dot(hb, w_ref[:, 2 * a:3 * a])
    k_ref[...] = k_att * _head_rms_scale(k_att, ones_bd) * kg_ref[...]


def _inproj(xp, xs, mod3, tabs, g_norm1, w_in_bf, q_norm_g, k_norm_g,
            mu_prev, mu_next, w0, w_up, a0, a_up, g_up, k_k, k_a, r_k):
    n_prompt_tiles = xp.shape[0] // ROW_TILE
    ntok = xp.shape[0] + xs.shape[0]
    nt = ntok // ROW_TILE
    c = RWKV_WIDTH
    per8 = ROW_TILE // 8
    row = lambda i, *_: (i, 0)
    const = lambda i, *_: (0, 0)
    resident = lambda shape: pl.BlockSpec(shape, const, pipeline_mode=pl.Buffered(1))
    xp_spec, xs_spec = _two_group_specs(n_prompt_tiles)

    def halo(n_rows, tile_of, side):
        def index(i, *_):
            t = tile_of(i)
            if side < 0:
                return (jnp.maximum(t * per8 - 1, 0), 0)
            return (jnp.minimum((t + 1) * per8, n_rows // 8 - 1), 0)
        return pl.BlockSpec((8, D_MODEL), index)

    prompt_tile = lambda i: jnp.minimum(i, n_prompt_tiles - 1)
    sample_tile = lambda i: jnp.maximum(i - n_prompt_tiles, 0)
    n_heads = ATTN_WIDTH // HEAD_DIM
    per_head = lambda g: jnp.tile(g.reshape(1, HEAD_DIM), (1, n_heads))
    vec = lambda n: pl.BlockSpec((1, n), const)
    grid_spec = pltpu.PrefetchScalarGridSpec(
        num_scalar_prefetch=3,
        grid=(nt,),
        in_specs=[xp_spec, xs_spec,
                  halo(xp.shape[0], prompt_tile, -1), halo(xp.shape[0], prompt_tile, +1),
                  halo(xs.shape[0], sample_tile, -1), halo(xs.shape[0], sample_tile, +1),
                  vec(D_MODEL),
                  pl.BlockSpec((1, 1, D_MODEL), lambda i, mr, *_: (mr[i] * 6 + 0, 0, 0)),
                  pl.BlockSpec((1, 1, D_MODEL), lambda i, mr, *_: (mr[i] * 6 + 1, 0, 0)),
                  resident((D_MODEL, IN_WIDTH)), vec(ATTN_WIDTH), vec(ATTN_WIDTH), resident((ATTN_WIDTH, ATTN_WIDTH)),
                  vec(RWKV_IN), vec(RWKV_IN), vec(2 * c), resident((128, 2 * c)), vec(2 * c), resident((128, 2 * c)),
                  resident((128, c)), vec(c), vec(c), vec(c)],
        out_specs=[pl.BlockSpec((ROW_TILE, ATTN_WIDTH), row)] * 3 + [pl.BlockSpec((ROW_TILE, c), row)] * 5
                  + [pl.BlockSpec((2, ROW_TILE, c), lambda i, *_: (0, i, 0))] * 3,
    )
    tok = lambda dt: jax.ShapeDtypeStruct((ntok, c), dt)
    tok2 = lambda dt: jax.ShapeDtypeStruct((2, ntok, c), dt)
    return pl.pallas_call(
        functools.partial(_inproj_kernel, n_prompt_tiles),
        grid_spec=grid_spec,
        out_shape=[jax.ShapeDtypeStruct((ntok, ATTN_WIDTH), F32)] * 3
                  + [tok(BF16), tok(BF16), tok(BF16), tok(F32), tok(F32), tok2(F32), tok2(BF16), tok2(BF16)],
        compiler_params=pltpu.CompilerParams(dimension_semantics=("arbitrary",), vmem_limit_bytes=VMEM_LIMIT),
        name="inproj",
    )(tabs["mrow_tile"], tabs["first"], tabs["last"], xp, xs, xp, xp, xs, xs,
      g_norm1.reshape(1, -1), mod3, mod3, w_in_bf, per_head(q_norm_g), per_head(k_norm_g), _head_ones(),
      mu_prev.reshape(1, -1), mu_next.reshape(1, -1), w0.reshape(1, -1), _block_diag2(w_up).astype(BF16),
      a0.reshape(1, -1), _block_diag2(a_up).astype(BF16), g_up.astype(BF16),
      k_k.reshape(1, -1), k_a.reshape(1, -1), r_k.reshape(1, -1))


def _lane_is_head0(shape):
    return lax.broadcasted_iota(jnp.int32, shape, len(shape) - 1) < HEAD_DIM


def _attn_kernel(hasprev_ref, hasnext_ref,
                 q_ref, kp_ref, kc_ref, kx_ref, vp_ref, vc_ref, vx_ref, beta_ref, bias_ref,
                 o_ref,
                 qd, kd, vd, tq, tk, tv, acc_o, acc_m):
    i = pl.program_id(0)
    has_prev = hasprev_ref[i]
    has_next = hasnext_ref[i]
    rows = 256

    m0_o = _lane_is_head0((HALF, 128))
    row16 = lax.broadcasted_iota(jnp.int32, (16, 128), 0)
    ones3 = jnp.where(row16 < 3, 1.0, 0.0).astype(BF16)
    ones_rows = jnp.concatenate([jnp.zeros((16, 128), BF16), ones3], axis=1)
    zeros_v = jnp.zeros((3 * HALF, 128), BF16)

    for di, dil in enumerate(DILATIONS):
        lq = SEQ_BLOCK // dil
        lk = lq + 2 * HALF
        nbq = lq // HALF
        pbase = HALO - HALF * dil

        if dil == 1:
            def cast_rows(src_ref, src_off, dst_ref, dst_off, nrows, chunk):
                def body(t, carry):
                    r0 = pl.multiple_of(t * chunk, chunk)
                    dst_ref[pl.ds(dst_off + r0, chunk), :] = src_ref[pl.ds(src_off + r0, chunk), :].astype(BF16)
                    return carry
                lax.fori_loop(0, nrows // chunk, body, 0)
            cast_rows(q_ref, 0, qd, 0, lq, rows)
            for prev_ref, cur_ref, next_ref, dst in ((kp_ref, kc_ref, kx_ref, kd), (vp_ref, vc_ref, vx_ref, vd)):
                cast_rows(prev_ref, pbase, dst, 0, HALF, HALF)
                cast_rows(cur_ref, 0, dst, HALF, lq, rows)
                cast_rows(next_ref, 0, dst, HALF + lq, HALF, HALF)
        elif dil == 16:
            assert pbase == 0 and lk * dil == SEQ_BLOCK + 2 * HALO
            wq, wk = SEQ_BLOCK // 4, (SEQ_BLOCK + 2 * HALO) // 4

            def split4(r0, carry, wq=wq, wk=wk):
                tq[pl.ds(pl.multiple_of(r0 * wq, HALF), wq), :] = q_ref[pl.ds(r0, wq, stride=4), :]
                base = pl.multiple_of(r0 * wk, HALF)
                for prev_ref, cur_ref, next_ref, dst in ((kp_ref, kc_ref, kx_ref, tk), (vp_ref, vc_ref, vx_ref, tv)):
                    dst[pl.ds(base, HALO // 4), :] = prev_ref[pl.ds(r0, HALO // 4, stride=4), :]
                    dst[pl.ds(base + HALO // 4, wq), :] = cur_ref[pl.ds(r0, wq, stride=4), :]
                    dst[pl.ds(base + HALO // 4 + wq, HALO // 4), :] = next_ref[pl.ds(r0, HALO // 4, stride=4), :]
                return carry
            lax.fori_loop(0, 4, split4, 0)

            def regroup16(r, carry, lq=lq, lk=lk, wq=wq, wk=wk):
                r0 = r & 3
                r1 = r >> 2
                qd[pl.ds(pl.multiple_of(r * lq, HALF), lq), :] = tq[pl.ds(r0 * wq + r1, lq, stride=4), :].astype(BF16)
                ko = pl.multiple_of(r * lk, HALF)
                kd[pl.ds(ko, lk), :] = tk[pl.ds(r0 * wk + r1, lk, stride=4), :].astype(BF16)
                vd[pl.ds(ko, lk), :] = tv[pl.ds(r0 * wk + r1, lk, stride=4), :].astype(BF16)
                return carry
            lax.fori_loop(0, dil, regroup16, 0)
        else:
            def regroup(r, carry, dil=dil, lq=lq, lk=lk, pbase=pbase):
                qd[pl.ds(pl.multiple_of(r * lq, HALF), lq), :] = q_ref[pl.ds(r, lq, stride=dil), :].astype(BF16)
                ko = pl.multiple_of(r * lk, HALF)
                for prev_ref, cur_ref, next_ref, dst in ((kp_ref, kc_ref, kx_ref, kd), (vp_ref, vc_ref, vx_ref, vd)):
                    dst[pl.ds(ko, HALF), :] = prev_ref[pl.ds(pbase + r, HALF, stride=dil), :].astype(BF16)
                    dst[pl.ds(ko + HALF, lq), :] = cur_ref[pl.ds(r, lq, stride=dil), :].astype(BF16)
                    dst[pl.ds(ko + HALF + lq, HALF), :] = next_ref[pl.ds(r, HALF, stride=dil), :].astype(BF16)
                return carry
            lax.fori_loop(0, dil, regroup, 0)

        shift = nbq.bit_length() - 1

        def bands(it, carry, di=di, dil=dil, nbq=nbq, shift=shift):
            work = []
            for u in range(BANDS_PER_STEP):
                idx = it * BANDS_PER_STEP + u
                r = idx >> shift
                b = idx & (nbq - 1)
                qo = pl.multiple_of(idx * HALF, HALF)
                ko = pl.multiple_of(idx * HALF + r * (2 * HALF), HALF)
                variant = jnp.where(jnp.logical_and(b == 0, has_prev == 0), 1,
                                    jnp.where(jnp.logical_and(b == nbq - 1, has_next == 0), 2, 0))
                work.append((r + (dil * HALF) * b, di * 3 + variant, qd[pl.ds(qo, HALF), :],
                             kd[pl.ds(ko, 3 * HALF), :], vd[pl.ds(ko, 3 * HALF), :]))
            s, m, o = [], [], []
            for w in work:
                si = _dot_nt(w[3], _stack_heads(w[2])) + bias_ref[w[1]]
                s.append(si)
                m.append(jnp.max(si, axis=0, keepdims=True))
            for w, si, mi in zip(work, s, m):
                pi = jnp.exp2(si - mi)
                li = jnp.sum(pi, axis=0, keepdims=True)
                hi, mid, lo = _split3(mi + jnp.log2(li))
                terms = jnp.where(row16 == 0, hi.astype(F32),
                                  jnp.where(row16 == 1, mid.astype(F32), jnp.where(row16 == 2, lo.astype(F32), 0.0)))
                lhs = jnp.concatenate([(pi * (1.0 / li)).astype(BF16), terms.astype(BF16)], axis=0)
                rhs = jnp.concatenate([jnp.concatenate([w[4], zeros_v], axis=1), ones_rows], axis=0)
                o.append(_dot_tn(lhs, rhs))
            for w, oi in zip(work, o):
                if dil == 1:
                    dst = pl.ds(pl.multiple_of(w[0], HALF), HALF)
                else:
                    dst = pl.ds(w[0], HALF, stride=dil)
                acc_o[di][dst, :] = jnp.where(m0_o, oi[0:HALF, 0:128], oi[HALF:2 * HALF, 0:128])
                acc_m[di][dst, :] = jnp.where(m0_o, oi[0:HALF, 128:256], oi[HALF:2 * HALF, 128:256])
            return carry

        lax.fori_loop(0, SEQ_BLOCK // HALF // BANDS_PER_STEP, bands, 0)

    beta = beta_ref[...]

    def merge(t, carry):
        r0 = pl.multiple_of(t * rows, rows)
        sl = pl.ds(r0, rows)
        ms = [acc_m[d][sl, :] for d in range(3)]
        mx = jnp.maximum(jnp.maximum(ms[0], ms[1]), ms[2])
        num = jnp.zeros((rows, 128), F32)
        den = jnp.zeros((rows, 128), F32)
        for d in range(3):
            e = jnp.exp2(ms[d] - mx)
            num = num + e * acc_o[d][sl, :]
            den = den + e
        o_ref[sl, :] = ((num / den) * beta).astype(o_ref.dtype)
        return carry

    lax.fori_loop(0, SEQ_BLOCK // rows, merge, 0)


def _attn_bias_table():
    qi = np.arange(HALF)[:, None]
    kj = np.arange(3 * HALF)[None, :]
    rel = np.abs(kj - HALF - qi).astype(np.float32)
    n_heads = ATTN_WIDTH // HEAD_DIM
    slopes = 2.0 ** (-8.0 * (np.arange(n_heads, dtype=np.float32) + 1.0) / n_heads)
    keep = [rel <= HALF, (rel <= HALF) & (kj >= HALF), (rel <= HALF) & (kj < 2 * HALF)]
    tab = np.empty((N_PAIRS, 3, 3, 2, HALF, 3 * HALF), np.float32)
    for h in range(n_heads):
        for di, dil in enumerate(DILATIONS):
            for var in range(3):
                tab[h // 2, di, var, h % 2] = np.where(keep[var], -(slopes[h] * (dil * rel)) * LOG2_E, NEG_INF)
    return jnp.asarray(tab.reshape(N_PAIRS * 9, 2 * HALF, 3 * HALF).transpose(0, 2, 1))


def _attention(q, k, v, has_prev, has_next, attn_beta):
    ntok = q.shape[0]
    nb = ntok // SEQ_BLOCK
    hb = SEQ_BLOCK // HALO
    cur = lambda i, j, hp, hn: (i, j)
    prev = lambda i, j, hp, hn: (jnp.maximum(i * hb - 1, 0), j)
    nxt = lambda i, j, hp, hn: (jnp.minimum((i + 1) * hb, nb * hb - 1), j)
    blk = pl.BlockSpec((SEQ_BLOCK, 128), cur)
    halo_p = pl.BlockSpec((HALO, 128), prev)
    halo_n = pl.BlockSpec((HALO, 128), nxt)
    win = SEQ_BLOCK + 2 * HALO
    grid_spec = pltpu.PrefetchScalarGridSpec(
        num_scalar_prefetch=2,
        grid=(nb, N_PAIRS),
        in_specs=[blk, halo_p, blk, halo_n, halo_p, blk, halo_n,
                  pl.BlockSpec((1, 128), lambda i, j, hp, hn: (0, j)),
                  pl.BlockSpec((9, 3 * HALF, 2 * HALF), lambda i, j, hp, hn: (j, 0, 0))],
        out_specs=blk,
        scratch_shapes=[pltpu.VMEM((SEQ_BLOCK, 128), BF16), pltpu.VMEM((win, 128), BF16), pltpu.VMEM((win, 128), BF16),
                        pltpu.VMEM((SEQ_BLOCK, 128), F32), pltpu.VMEM((win, 128), F32), pltpu.VMEM((win, 128), F32),
                        [pltpu.VMEM((SEQ_BLOCK, 128), F32)] * 3,
                        [pltpu.VMEM((SEQ_BLOCK, 128), F32)] * 3],
    )
    return pl.pallas_call(
        _attn_kernel,
        grid_spec=grid_spec,
        out_shape=jax.ShapeDtypeStruct((ntok, ATTN_WIDTH), BF16),
        compiler_params=pltpu.CompilerParams(dimension_semantics=("arbitrary", "arbitrary"),
                                             vmem_limit_bytes=VMEM_LIMIT),
        name="attn",
    )(has_prev, has_next, q, k, k, k, v, v, v, attn_beta.reshape(1, -1), _attn_bias_table())


def _block_diag2(m):
    z = jnp.zeros_like(m[0])
    return jnp.concatenate([jnp.concatenate([m[0], z], axis=1), jnp.concatenate([z, m[1]], axis=1)], axis=0)


def _head_ones():
    idx = np.arange(RWKV_WIDTH) // HEAD_DIM
    return jnp.asarray((idx[:, None] == idx[None, :]).astype(np.float32), dtype=BF16)


def _stack_heads(x):
    m0 = _lane_is_head0(x.shape)
    z = jnp.zeros_like(x)
    return jnp.concatenate([jnp.where(m0, x, z), jnp.where(m0, z, x)], axis=0)


def _stack_heads_bf16(x):
    return _stack_heads(x.astype(BF16))


def _scan_masks(rev):
    c = CHUNK
    rowi = lax.broadcasted_iota(jnp.int32, (c, 128), 0)
    colj = lax.broadcasted_iota(jnp.int32, (c, 128), 1) & (c - 1)
    ti = lax.broadcasted_iota(jnp.int32, (c, c), 0)
    tj = lax.broadcasted_iota(jnp.int32, (c, c), 1)
    if rev:
        strict, incl, tri = colj > rowi, colj >= rowi, tj >= ti
    else:
        strict, incl, tri = colj < rowi, colj <= rowi, tj <= ti
    return dict(strict=strict, incl=incl, tri=tri.astype(BF16), eye2=(colj == rowi).astype(F32))


def _scan_pre(ins, masks):
    c = CHUNK
    idx = range(len(ins))
    lw = [a[0] for a in ins]
    v = [a[3] for a in ins]

    cl = []
    for i in idx:
        hi, mid, _ = _split3(lw[i])
        cl2 = _dot(masks[i]["tri"], jnp.concatenate([hi, mid], axis=1))
        cl.append(cl2[:, 0:128] + cl2[:, 128:256])
    yield
    g_tot = [jnp.exp(cl[i][0:1, :] if masks[i]["rev"] else cl[i][c - 1:c, :]) for i in idx]
    g_inv = [jnp.exp(-cl[i]) for i in idx]
    rt = [ins[i][1] * jnp.exp(cl[i]) for i in idx]
    bt = [ins[i][2] * jnp.exp(cl[i] - lw[i]) for i in idx]
    at = [-(ins[i][5] * g_inv[i]) for i in idx]
    kt = [ins[i][4] * g_inv[i] for i in idx]
    br = [jnp.concatenate([bt[i], rt[i]], axis=0).astype(BF16) for i in idx]
    lm = [_dot_nt(br[i], jnp.concatenate([_stack_heads_bf16(at[i]), _stack_heads_bf16(kt[i])], axis=0))
          for i in idx]
    akg = [jnp.concatenate([at[i] * g_tot[i], kt[i] * g_tot[i]], axis=0).astype(BF16) for i in idx]
    yield
    la = [jnp.where(masks[i]["strict"], lm[i][0:c, 0:128], 0.0) for i in idx]
    lk = [jnp.where(masks[i]["strict"], lm[i][0:c, 128:256], 0.0).astype(BF16) for i in idx]
    mam = [jnp.concatenate([jnp.where(masks[i]["incl"], lm[i][c:2 * c, 0:128], 0.0),
                            jnp.where(masks[i]["incl"], lm[i][c:2 * c, 128:256], 0.0)], axis=1).astype(BF16)
           for i in idx]

    sv = [_stack_heads_bf16(v[i]) for i in idx]
    lkv = [_dot(lk[i], sv[i]) for i in idx]

    t = [masks[i]["eye2"] + la[i] for i in idx]
    pw = [_dot(la[i].astype(BF16), _stack_heads_bf16(la[i])) for i in idx]
    yield
    for step in range(5):
        if step < 4:
            out = [_dot(pw[i].astype(BF16),
                        jnp.concatenate([_stack_heads_bf16(pw[i]), _stack_heads_bf16(t[i])], axis=1))
                   for i in idx]
            t = [t[i] + out[i][:, 128:256] for i in idx]
            pw = [out[i][:, 0:128] for i in idx]
        else:
            t = [t[i] + _dot(pw[i].astype(BF16), _stack_heads_bf16(t[i])) for i in idx]
        yield
    return [dict(br=br[i], lkv=lkv[i], mam=mam[i], sv=sv[i], t=t[i].astype(BF16), v=v[i].astype(BF16),
                 akg=akg[i], g_tot=g_tot[i]) for i in idx]


def _scan_seq(pre, zs, n_chunks):
    c = CHUNK
    per = len(zs)
    idx = range(per)
    zr = lax.broadcasted_iota(jnp.int32, (128, 128), 0) < HEAD_DIM
    zc = lax.broadcasted_iota(jnp.int32, (128, 128), 1) < HEAD_DIM
    same_head = zr == zc
    ys = []
    for k in range(n_chunks):
        p = pre[k * per:(k + 1) * per]
        brz = [_dot_nt(p[i]["br"], zs[i].astype(BF16)) for i in idx]
        yield
        u = [_dot(p[i]["t"], _stack_heads_bf16(brz[i][0:c] + p[i]["lkv"])) for i in idx]
        yield
        ys += [brz[i][c:2 * c] + _dot(p[i]["mam"], jnp.concatenate([_stack_heads_bf16(u[i]), p[i]["sv"]], axis=0))
               for i in idx]
        zs = [zs[i] * p[i]["g_tot"]
              + jnp.where(same_head, _dot_tn(jnp.concatenate([u[i].astype(BF16), p[i]["v"]], axis=0), p[i]["akg"]), 0.0)
              for i in idx]
        yield
    return ys, zs


def _run(gen):
    try:
        while True:
            next(gen)
    except StopIteration as done:
        return done.value


def _interleave(gen_a, gen_b):
    live = {"a": gen_a, "b": gen_b}
    result = {}
    while live:
        for name in list(live):
            try:
                next(live[name])
            except StopIteration as done:
                result[name] = done.value
                del live[name]
    return result["a"], result["b"]


def _scan_kernel(bmap_ref, reset_ref,
                 rf, kkf, vf, lwf, kdf, bdf, rb, kkb, vb, lwb, kdb, bdb,
                 yf_ref, yb_ref, zf, zb):
    del bmap_ref
    g = pl.program_id(0)

    @pl.when(reset_ref[g] == 1)
    def _():
        zf[...] = jnp.zeros_like(zf)
        zb[...] = jnp.zeros_like(zb)

    nchunk = SCAN_BLOCK // CHUNK
    n_groups = nchunk // CHUNKS_PER_GROUP
    mask_f = dict(_scan_masks(False), rev=False)
    mask_b = dict(_scan_masks(True), rev=True)
    dirs = ((mask_f, (lwf, rf, kkf, vf, kdf, bdf), yf_ref, zf),
            (mask_b, (lwb, rb, kkb, vb, kdb, bdb), yb_ref, zb))

    def group_items(gi):
        ins, masks, outs = [], [], []
        for k in range(CHUNKS_PER_GROUP):
            step = gi * CHUNKS_PER_GROUP + k
            for mask, refs, y_ref, _ in dirs:
                cc = (nchunk - 1 - step) if mask["rev"] else step
                rows = slice(cc * CHUNK, (cc + 1) * CHUNK)
                for pr in range(N_PAIRS):
                    lanes = slice(128 * pr, 128 * (pr + 1))
                    ins.append([ref[rows, lanes] for ref in refs])
                    masks.append(mask)
                    outs.append((y_ref, rows, lanes))
        return ins, masks, outs

    zs = [z_ref[pr] for _, _, _, z_ref in dirs for pr in range(N_PAIRS)]
    ins, masks, outs = group_items(0)
    pre = _run(_scan_pre(ins, masks))
    for gi in range(n_groups):
        seq = _scan_seq(pre, zs, CHUNKS_PER_GROUP)
        if gi + 1 < n_groups:
            ins, masks, next_outs = group_items(gi + 1)
            pre, (ys, zs) = _interleave(_scan_pre(ins, masks), seq)
        else:
            ys, zs = _run(seq)
        for (y_ref, rows, lanes), y in zip(outs, ys):
            y_ref[rows, lanes] = y
        if gi + 1 < n_groups:
            outs = next_outs
    state_refs = [(z_ref, pr) for _, _, _, z_ref in dirs for pr in range(N_PAIRS)]
    for (z_ref, pr), z in zip(state_refs, zs):
        z_ref[pr] = z


def _scan(r, kk, v, lw, kd, bd, bmap_bwd, reset):
    ntok = r.shape[0]
    c = RWKV_WIDTH
    ns = ntok // SCAN_BLOCK
    fwd = lambda g, bm, rs: (g, 0)
    bwd = lambda g, bm, rs: (bm[g], 0)
    fwd2 = lambda g, bm, rs: (0, g, 0)
    bwd2 = lambda g, bm, rs: (1, bm[g], 0)
    tokf = pl.BlockSpec((SCAN_BLOCK, c), fwd)
    tokb = pl.BlockSpec((SCAN_BLOCK, c), bwd)
    dirf = pl.BlockSpec((None, SCAN_BLOCK, c), fwd2)
    dirb = pl.BlockSpec((None, SCAN_BLOCK, c), bwd2)
    grid_spec = pltpu.PrefetchScalarGridSpec(
        num_scalar_prefetch=2,
        grid=(ns,),
        in_specs=[tokf, tokf, tokf, dirf, dirf, dirf, tokb, tokb, tokb, dirb, dirb, dirb],
        out_specs=[tokf, tokb],
        scratch_shapes=[pltpu.VMEM((N_PAIRS, 128, 128), F32), pltpu.VMEM((N_PAIRS, 128, 128), F32)],
    )
    tok = jax.ShapeDtypeStruct((ntok, c), F32)
    return pl.pallas_call(
        _scan_kernel,
        grid_spec=grid_spec,
        out_shape=[tok, tok],
        compiler_params=pltpu.CompilerParams(dimension_semantics=("arbitrary",), vmem_limit_bytes=VMEM_LIMIT),
        name="rwkv_scan",
    )(bmap_bwd, reset, r, kk, v, lw, kd, bd, r, kk, v, lw, kd, bd)


def _post_ffn_kernel(n_prompt_tiles, mrow_ref, xp_ref, xs_ref, yf_ref, yb_ref, bonus_ref, g_ref, attn_ref,
                     gt1_ref, sh2_ref, sc2_ref, gt2_ref, g2_ref, lnw_ref, lnb_ref, ones_ref, wo_ref, w1_ref, w2_ref,
                     op_ref, os_ref):
    del mrow_ref
    is_prompt = pl.program_id(0) < n_prompt_tiles
    x = jnp.where(is_prompt, xp_ref[...], xs_ref[...])

    y = yf_ref[...] + yb_ref[...]
    ones_bd = ones_ref[...]
    mu = _dot(y.astype(BF16), ones_bd) * (1.0 / HEAD_DIM)
    d = y - mu
    var = _dot((d * d).astype(BF16), ones_bd) * (1.0 / HEAD_DIM)
    yn = d * lax.rsqrt(var + LN_X_EPS) * lnw_ref[...] + lnb_ref[...]
    rw = ((yn + bonus_ref[...]) * g_ref[...]).astype(BF16)

    a = ATTN_WIDTH
    mix = _dot(attn_ref[...], wo_ref[0:a, :]) + _dot(rw, wo_ref[a:, :])
    x1 = x + gt1_ref[0] * mix
    h = _modulated_rmsnorm(x1, g2_ref[...], sc2_ref[0], sh2_ref[0]).astype(BF16)

    acc = jnp.zeros((h.shape[0], D_MODEL), F32)
    for j in range(D_FF // FF_STEP):
        cols = slice(j * FF_STEP, (j + 1) * FF_STEP)
        act = jnp.maximum(_dot(h, w1_ref[:, cols]), 0.0)
        acc = acc + _dot((act * act).astype(BF16), w2_ref[cols, :])
    out = x1 + gt2_ref[0] * acc

    @pl.when(is_prompt)
    def _():
        op_ref[...] = out

    @pl.when(jnp.logical_not(is_prompt))
    def _():
        os_ref[...] = out


def _post_ffn(xp, xs, yf, yb, bonus, g, attn, mod3, mrow_tile, g_norm2, ln_x_w, ln_x_b, w_out_bf, w1_bf, w2_bf):
    n_prompt_tiles = xp.shape[0] // ROW_TILE
    nt = (xp.shape[0] + xs.shape[0]) // ROW_TILE
    c = RWKV_WIDTH
    row = lambda i, mr: (i, 0)
    const = lambda i, mr: (0, 0)
    resident = lambda shape: pl.BlockSpec(shape, const, pipeline_mode=pl.Buffered(1))
    modspec = lambda j: pl.BlockSpec((1, 1, D_MODEL), lambda i, mr: (mr[i] * 6 + j, 0, 0))
    xp_spec, xs_spec = _two_group_specs(n_prompt_tiles)
    half = pl.BlockSpec((ROW_TILE, c), row)
    grid_spec = pltpu.PrefetchScalarGridSpec(
        num_scalar_prefetch=1,
        grid=(nt,),
        in_specs=[xp_spec, xs_spec, half, half, half, half, half,
                  modspec(2), modspec(3), modspec(4), modspec(5),
                  pl.BlockSpec((1, D_MODEL), const), pl.BlockSpec((1, c), const), pl.BlockSpec((1, c), const),
                  resident((c, c)), resident((D_MODEL, D_MODEL)), resident((D_MODEL, D_FF)),
                  resident((D_FF, D_MODEL))],
        out_specs=[xp_spec, xs_spec],
    )
    return pl.pallas_call(
        functools.partial(_post_ffn_kernel, n_prompt_tiles),
        grid_spec=grid_spec,
        out_shape=[jax.ShapeDtypeStruct(xp.shape, F32), jax.ShapeDtypeStruct(xs.shape, F32)],
        compiler_params=pltpu.CompilerParams(dimension_semantics=("arbitrary",), vmem_limit_bytes=VMEM_LIMIT),
        name="post_ffn",
    )(mrow_tile, xp, xs, yf, yb, bonus, g, attn, mod3, mod3, mod3, mod3, g_norm2.reshape(1, -1),
      ln_x_w.reshape(1, -1), ln_x_b.reshape(1, -1), _head_ones(), w_out_bf, w1_bf, w2_bf)


def _sequence_tables(seq_blocks):
    mod_row, has_prev, has_next = [], [], []
    for s, n in enumerate(seq_blocks):
        for b in range(n):
            mod_row.append(s)
            has_prev.append(int(b > 0))
            has_next.append(int(b < n - 1))
    per = SEQ_BLOCK // SCAN_BLOCK
    bmap_bwd, reset = [], []
    start = 0
    for n in seq_blocks:
        ns = n * per
        for t in range(ns):
            bmap_bwd.append(start + ns - 1 - t)
            reset.append(int(t == 0))
        start += ns
    i32 = lambda a: jnp.asarray(np.asarray(a, np.int32))
    rep = lambda a, k: np.repeat(np.asarray(a, np.int32), k)
    row_tiles = SEQ_BLOCK // ROW_TILE
    first = np.zeros(len(mod_row) * row_tiles, np.int32)
    last = np.zeros(len(mod_row) * row_tiles, np.int32)
    for b in range(len(mod_row)):
        if not has_prev[b]:
            first[b * row_tiles] = 1
        if not has_next[b]:
            last[(b + 1) * row_tiles - 1] = 1
    return dict(mrow_tile=i32(rep(mod_row, row_tiles)), has_prev=i32(has_prev), has_next=i32(has_next),
                bmap_bwd=i32(bmap_bwd), reset=i32(reset), first=i32(first), last=i32(last))


def _layer(xp, xs, c_all, tabs, w_ada, b_ada, g_norm1, g_norm2, w_in, q_norm_g, k_norm_g, attn_beta,
           mu_prev, mu_next, w0, w_up, a0, a_up, g_up, k_k, k_a, r_k, ln_x_w, ln_x_b, w_out, w_ff1, w_ff2):
    mod = _ada(c_all, w_ada, b_ada)
    mod3 = mod.reshape(-1, 1, D_MODEL)
    q, k, v, r, kk, vv, g, bonus, lw, kd, bd = _inproj(
        xp, xs, mod3, tabs, g_norm1, w_in.astype(BF16), q_norm_g, k_norm_g,
        mu_prev, mu_next, w0, w_up, a0, a_up, g_up, k_k, k_a, r_k)
    attn = _attention(q, k, v, tabs["has_prev"], tabs["has_next"], attn_beta)
    yf, yb = _scan(r, kk, vv, lw, kd, bd, tabs["bmap_bwd"], tabs["reset"])
    return _post_ffn(xp, xs, yf, yb, bonus, g, attn, mod3, tabs["mrow_tile"], g_norm2, ln_x_w, ln_x_b,
                     w_out.astype(BF16), w_ff1.astype(BF16), w_ff2.astype(BF16))


def kernel(x_prompt, x_sample, c_prompt, c_sample, w_ada, b_ada, g_norm1, g_norm2, w_in, q_norm_g, k_norm_g, attn_beta, mu_prev, mu_next, w0, w_up, a0, a_up, g_up, k_k, k_a, r_k, ln_x_w, ln_x_b, w_out, w_ff1, w_ff2):
    bp, sp, d = x_prompt.shape
    bs, ss, _ = x_sample.shape
    assert d == D_MODEL and sp % SEQ_BLOCK == 0 and ss % SEQ_BLOCK == 0
    seq_blocks = [sp // SEQ_BLOCK] * bp + [ss // SEQ_BLOCK] * bs
    tabs = _sequence_tables(seq_blocks)
    n_seq = bp + bs
    pad = (-n_seq) % 8
    c_all = jnp.concatenate([c_prompt, c_sample, jnp.zeros((pad, d), F32)], axis=0)
    yp = x_prompt.reshape(bp * sp, d)
    ys = x_sample.reshape(bs * ss, d)
    for i in range(w_ada.shape[0]):
        yp, ys = _layer(yp, ys, c_all, tabs, w_ada[i], b_ada[i], g_norm1[i], g_norm2[i], w_in[i], q_norm_g[i],
                        k_norm_g[i], attn_beta[i], mu_prev[i], mu_next[i], w0[i], w_up[i], a0[i], a_up[i], g_up[i],
                        k_k[i], k_a[i], r_k[i], ln_x_w[i], ln_x_b[i], w_out[i], w_ff1[i], w_ff2[i])
    return (yp.reshape(bp, sp, d), ys.reshape(bs, ss, d))
```

```python
import functools

import numpy as np
import jax
import jax.numpy as jnp
from jax import lax
from jax.experimental import pallas as pl
from jax.experimental.pallas import tpu as pltpu

F32 = jnp.float32
BF16 = jnp.bfloat16

D_MODEL = 1024
HEAD_DIM = 64
ATTN_WIDTH = 512
RWKV_WIDTH = 512
N_PAIRS = ATTN_WIDTH // 128
RWKV_IN = 1920
IN_WIDTH = 3 * ATTN_WIDTH + RWKV_IN
D_FF = 4096
DILATIONS = (1, 4, 16)
HALF = 64
NORM_EPS = 1e-6
LN_X_EPS = 64e-5
NEG_INF = -1e30
LOG2_E = 1.4426950408889634

SEQ_BLOCK = 2048
HALO = 1024
SCAN_BLOCK = 512
CHUNK = 64
CHUNKS_PER_GROUP = 2
ROW_TILE = 512
BANDS_PER_BATCH = 8
BANDS_PER_STEP = 32
FF_STEP = 1024

VMEM_LIMIT = 56 * 1024 * 1024


def _dot(a, b):
    return jnp.dot(a, b, preferred_element_type=F32)


def _dot_nt(a, b):
    return lax.dot_general(a, b, (((1,), (1,)), ((), ())), preferred_element_type=F32)


def _dot_tn(a, b):
    return lax.dot_general(a, b, (((0,), (0,)), ((), ())), preferred_element_type=F32)


def _split3(x):
    hi = x.astype(BF16)
    r1 = x - hi.astype(F32)
    mid = r1.astype(BF16)
    lo = (r1 - mid.astype(F32)).astype(BF16)
    return hi, mid, lo


def _sigmoid(x):
    return 1.0 / (1.0 + jnp.exp(-x))


def _ada_kernel(c_ref, w_ref, b_ref, o_ref):
    c = c_ref[...]
    s = c * _sigmoid(c)
    s1, s2, _ = _split3(s)
    w = w_ref[...]
    w1, w2, _ = _split3(w)
    o_ref[...] = _dot(s1, w1) + (_dot(s1, w2) + _dot(s2, w1)) + b_ref[...]


def _ada(c_all, w_ada, b_ada):
    n = c_all.shape[0]
    nt = w_ada.shape[1] // D_MODEL
    return pl.pallas_call(
        _ada_kernel,
        grid=(nt,),
        in_specs=[pl.BlockSpec((n, D_MODEL), lambda j: (0, 0)),
                  pl.BlockSpec((D_MODEL, D_MODEL), lambda j: (0, j)),
                  pl.BlockSpec((1, D_MODEL), lambda j: (0, j))],
        out_specs=pl.BlockSpec((n, D_MODEL), lambda j: (0, j)),
        out_shape=jax.ShapeDtypeStruct((n, w_ada.shape[1]), F32),
        compiler_params=pltpu.CompilerParams(dimension_semantics=("arbitrary",), vmem_limit_bytes=VMEM_LIMIT),
        name="ada",
    )(c_all, w_ada, b_ada.reshape(1, -1))


def _modulated_rmsnorm(x, g, scale, shift):
    ms = jnp.mean(x * x, axis=-1, keepdims=True)
    return (x * lax.rsqrt(ms + NORM_EPS) * g) * (1.0 + scale) + shift


def _two_group_specs(n_prompt_tiles):
    def prompt(i, *_):
        return (jnp.minimum(i, n_prompt_tiles - 1), 0)

    def sample(i, *_):
        return (jnp.maximum(i - n_prompt_tiles, 0), 0)
    return pl.BlockSpec((ROW_TILE, D_MODEL), prompt), pl.BlockSpec((ROW_TILE, D_MODEL), sample)


def _head_rms_scale(z, ones_bd):
    ssq = _dot((z * z).astype(BF16), ones_bd)
    return lax.rsqrt(ssq * (1.0 / HEAD_DIM) + NORM_EPS)


def _inproj_kernel(n_prompt_tiles, mrow_ref, first_ref, last_ref,
                   xp_ref, xs_ref, xpp_ref, xpn_ref, xsp_ref, xsn_ref,
                   g_ref, sh_ref, sc_ref, w_ref, qg_ref, kg_ref, ones_ref,
                   mup_ref, mun_ref, w0_ref, wup_ref, a0_ref, aup_ref, gup_ref, kk_ref, ka_ref, rk_ref,
                   q_ref, k_ref, v_ref,
                   r_out, kk_out, v_out, g_out, bonus_out, lw_out, kd_out, bd_out):
    del mrow_ref
    i = pl.program_id(0)
    is_prompt = i < n_prompt_tiles
    g1, sc, sh = g_ref[...], sc_ref[0], sh_ref[0]
    keep_prev = jnp.where(first_ref[i] == 0, 1.0, 0.0)
    keep_next = jnp.where(last_ref[i] == 0, 1.0, 0.0)
    h = _modulated_rmsnorm(jnp.where(is_prompt, xp_ref[...], xs_ref[...]), g1, sc, sh)
    h_prev = _modulated_rmsnorm(jnp.where(is_prompt, xpp_ref[...], xsp_ref[...]), g1, sc, sh) * keep_prev
    h_next = _modulated_rmsnorm(jnp.where(is_prompt, xpn_ref[...], xsn_ref[...]), g1, sc, sh) * keep_next

    a = ATTN_WIDTH
    hb = h.astype(BF16)
    ones_bd = ones_ref[...]
    q = _dot(hb, w_ref[:, 0:a])
    q_ref[...] = q * _head_rms_scale(q, ones_bd) * (qg_ref[...] * (HEAD_DIM ** -0.5 * LOG2_E))
    k = _dot(hb, w_ref[:, a:2 * a])
    k_ref[...] = k * _head_rms_scale(k, ones_bd) * kg_ref[...]
    v_ref[...] = _dot(hb, w_ref[:, 2 * a:3 * a])

    tm = h.shape[0]
    hext = jnp.concatenate([h_prev, h, h_next], axis=0).astype(BF16)
    next_amt = hext.shape[0] - 1

    def shifted(lo, hi):
        p = _dot(hext, w_ref[:, 3 * a + lo:3 * a + hi])
        prev = pltpu.roll(p, 1, 0)
        nxt = pltpu.roll(p, next_amt, 0)
        ps = p + mup_ref[:, lo:hi] * (prev - p) + mun_ref[:, lo:hi] * (nxt - p)
        return ps[8:8 + tm]

    c = RWKV_WIDTH
    r = shifted(0, c)
    k = shifted(c, 2 * c)
    v = shifted(2 * c, 3 * c)
    low = shifted(3 * c, 3 * c + 384)
    wd = low[:, 0:128]
    ad = low[:, 128:256]
    gd = low[:, 256:384]

    w_raw = w0_ref[...] + _dot(jnp.tanh(wd).astype(BF16), wup_ref[...])
    lw = (-float(np.exp(-0.5))) * _sigmoid(w_raw)
    a_rate = _sigmoid(a0_ref[...] + _dot(ad.astype(BF16), aup_ref[...]))
    gate = _dot(_sigmoid(gd).astype(BF16), gup_ref[...])

    kkv = k * kk_ref[...]
    ssq = _dot((kkv * kkv).astype(BF16), ones_bd)
    kkv = kkv * lax.rsqrt(jnp.maximum(ssq, 1e-24))
    ka = ka_ref[...]
    kd0 = k * (1.0 + (a_rate[:, 0:c] - 1.0) * ka)
    kd1 = k * (1.0 + (a_rate[:, c:2 * c] - 1.0) * ka)
    hsum = _dot((r * (kd0 + kd1) * rk_ref[...]).astype(BF16), ones_bd)

    r_out[...] = r.astype(BF16)
    kk_out[...] = kkv.astype(BF16)
    v_out[...] = v.astype(BF16)
    g_out[...] = gate
    bonus_out[...] = hsum * v
    lw_out[0] = lw[:, 0:c]
    lw_out[1] = lw[:, c:2 * c]
    kd_out[0] = kd0.astype(BF16)
    kd_out[1] = kd1.astype(BF16)
    bd_out[0] = (kkv * a_rate[:, 0:c]).astype(BF16)
    bd_out[1] = (kkv * a_rate[:, c:2 * c]).astype(BF16)


def _inproj(xp, xs, mod3, tabs, g_norm1, w_in_bf, q_norm_g, k_norm_g,
            mu_prev, mu_next, w0, w_up, a0, a_up, g_up, k_k, k_a, r_k):
    n_prompt_tiles = xp.shape[0] // ROW_TILE
    ntok = xp.shape[0] + xs.shape[0]
    nt = ntok // ROW_TILE
    c = RWKV_WIDTH
    per8 = ROW_TILE // 8
    row = lambda i, *_: (i, 0)
    const = lambda i, *_: (0, 0)
    resident = lambda shape: pl.BlockSpec(shape, const, pipeline_mode=pl.Buffered(1))
    xp_spec, xs_spec = _two_group_specs(n_prompt_tiles)

    def halo(n_rows, tile_of, side):
        def index(i, *_):
            t = tile_of(i)
            if side < 0:
                return (jnp.maximum(t * per8 - 1, 0), 0)
            return (jnp.minimum((t + 1) * per8, n_rows // 8 - 1), 0)
        return pl.BlockSpec((8, D_MODEL), index)

    prompt_tile = lambda i: jnp.minimum(i, n_prompt_tiles - 1)
    sample_tile = lambda i: jnp.maximum(i - n_prompt_tiles, 0)
    n_heads = ATTN_WIDTH // HEAD_DIM
    per_head = lambda g: jnp.tile(g.reshape(1, HEAD_DIM), (1, n_heads))
    vec = lambda n: pl.BlockSpec((1, n), const)
    grid_spec = pltpu.PrefetchScalarGridSpec(
        num_scalar_prefetch=3,
        grid=(nt,),
        in_specs=[xp_spec, xs_spec,
                  halo(xp.shape[0], prompt_tile, -1), halo(xp.shape[0], prompt_tile, +1),
                  halo(xs.shape[0], sample_tile, -1), halo(xs.shape[0], sample_tile, +1),
                  vec(D_MODEL),
                  pl.BlockSpec((1, 1, D_MODEL), lambda i, mr, *_: (mr[i] * 6 + 0, 0, 0)),
                  pl.BlockSpec((1, 1, D_MODEL), lambda i, mr, *_: (mr[i] * 6 + 1, 0, 0)),
                  resident((D_MODEL, IN_WIDTH)), vec(ATTN_WIDTH), vec(ATTN_WIDTH), resident((ATTN_WIDTH, ATTN_WIDTH)),
                  vec(RWKV_IN), vec(RWKV_IN), vec(2 * c), resident((128, 2 * c)), vec(2 * c), resident((128, 2 * c)),
                  resident((128, c)), vec(c), vec(c), vec(c)],
        out_specs=[pl.BlockSpec((ROW_TILE, ATTN_WIDTH), row)] * 3 + [pl.BlockSpec((ROW_TILE, c), row)] * 5
                  + [pl.BlockSpec((2, ROW_TILE, c), lambda i, *_: (0, i, 0))] * 3,
    )
    tok = lambda dt: jax.ShapeDtypeStruct((ntok, c), dt)
    tok2 = lambda dt: jax.ShapeDtypeStruct((2, ntok, c), dt)
    return pl.pallas_call(
        functools.partial(_inproj_kernel, n_prompt_tiles),
        grid_spec=grid_spec,
        out_shape=[jax.ShapeDtypeStruct((ntok, ATTN_WIDTH), F32)] * 3
                  + [tok(BF16), tok(BF16), tok(BF16), tok(F32), tok(F32), tok2(F32), tok2(BF16), tok2(BF16)],
        compiler_params=pltpu.CompilerParams(dimension_semantics=("arbitrary",), vmem_limit_bytes=VMEM_LIMIT),
        name="inproj",
    )(tabs["mrow_tile"], tabs["first"], tabs["last"], xp, xs, xp, xp, xs, xs,
      g_norm1.reshape(1, -1), mod3, mod3, w_in_bf, per_head(q_norm_g), per_head(k_norm_g), _head_ones(),
      mu_prev.reshape(1, -1), mu_next.reshape(1, -1), w0.reshape(1, -1), _block_diag2(w_up).astype(BF16),
      a0.reshape(1, -1), _block_diag2(a_up).astype(BF16), g_up.astype(BF16),
      k_k.reshape(1, -1), k_a.reshape(1, -1), r_k.reshape(1, -1))


def _lane_is_head0(shape):
    return lax.broadcasted_iota(jnp.int32, shape, len(shape) - 1) < HEAD_DIM


def _attn_kernel(hasprev_ref, hasnext_ref,
                 q_ref, kp_ref, kc_ref, kx_ref, vp_ref, vc_ref, vx_ref, beta_ref, bias_ref,
                 o_ref,
                 qd, kd, vd, tq, tk, tv, acc_o, acc_m):
    i = pl.program_id(0)
    has_prev = hasprev_ref[i]
    has_next = hasnext_ref[i]
    rows = 256

    m0_o = _lane_is_head0((HALF, 128))
    row16 = lax.broadcasted_iota(jnp.int32, (16, 128), 0)
    ones3 = jnp.where(row16 < 3, 1.0, 0.0).astype(BF16)
    ones_rows = jnp.concatenate([jnp.zeros((16, 128), BF16), ones3], axis=1)
    zeros_v = jnp.zeros((3 * HALF, 128), BF16)

    def cast_rows(src_ref, src_off, dst_ref, dst_off, nrows, chunk):
        def body(t, carry):
            r0 = pl.multiple_of(t * chunk, chunk)
            dst_ref[pl.ds(dst_off + r0, chunk), :] = src_ref[pl.ds(src_off + r0, chunk), :].astype(BF16)
            return carry
        lax.fori_loop(0, nrows // chunk, body, 0)

    wq, wk = SEQ_BLOCK // 4, (SEQ_BLOCK + 2 * HALO) // 4

    def split4(r0, carry):
        tq[pl.ds(pl.multiple_of(r0 * wq, HALF), wq), :] = q_ref[pl.ds(r0, wq, stride=4), :]
        base = pl.multiple_of(r0 * wk, HALF)
        for prev_ref, cur_ref, next_ref, dst in ((kp_ref, kc_ref, kx_ref, tk), (vp_ref, vc_ref, vx_ref, tv)):
            dst[pl.ds(base, HALO // 4), :] = prev_ref[pl.ds(r0, HALO // 4, stride=4), :]
            dst[pl.ds(base + HALO // 4, wq), :] = cur_ref[pl.ds(r0, wq, stride=4), :]
            dst[pl.ds(base + HALO // 4 + wq, HALO // 4), :] = next_ref[pl.ds(r0, HALO // 4, stride=4), :]
        return carry

    for di, dil in enumerate(DILATIONS):
        lq = SEQ_BLOCK // dil
        lk = lq + 2 * HALF
        nbq = lq // HALF
        pbase = HALO - HALF * dil

        if dil == 1:
            cast_rows(q_ref, 0, qd, 0, lq, rows)
            for prev_ref, cur_ref, next_ref, dst in ((kp_ref, kc_ref, kx_ref, kd), (vp_ref, vc_ref, vx_ref, vd)):
                cast_rows(prev_ref, pbase, dst, 0, HALF, HALF)
                cast_rows(cur_ref, 0, dst, HALF, lq, rows)
                cast_rows(next_ref, 0, dst, HALF + lq, HALF, HALF)
        elif dil == 4:
            assert lq == wq and pbase % 4 == 0
            lax.fori_loop(0, 4, split4, 0)
            cast_rows(tq, 0, qd, 0, SEQ_BLOCK, rows)
            for r0 in range(4):
                cast_rows(tk, r0 * wk + pbase // 4, kd, r0 * lk, lk, 2 * HALF)
                cast_rows(tv, r0 * wk + pbase // 4, vd, r0 * lk, lk, 2 * HALF)
        elif dil == 16:
            assert pbase == 0 and lk * dil == SEQ_BLOCK + 2 * HALO

            def regroup16(r, carry, lq=lq, lk=lk, wq=wq, wk=wk):
                r0 = r & 3
                r1 = r >> 2
                qd[pl.ds(pl.multiple_of(r * lq, HALF), lq), :] = tq[pl.ds(r0 * wq + r1, lq, stride=4), :].astype(BF16)
                ko = pl.multiple_of(r * lk, HALF)
                kd[pl.ds(ko, lk), :] = tk[pl.ds(r0 * wk + r1, lk, stride=4), :].astype(BF16)
                vd[pl.ds(ko, lk), :] = tv[pl.ds(r0 * wk + r1, lk, stride=4), :].astype(BF16)
                return carry
            lax.fori_loop(0, dil, regroup16, 0)
        else:
            raise NotImplementedError(f"dilation {dil}")

        shift = nbq.bit_length() - 1

        def bands(it, carry, di=di, dil=dil, nbq=nbq, shift=shift):
            work = []
            for u in range(BANDS_PER_STEP):
                idx = it * BANDS_PER_STEP + u
                r = idx >> shift
                b = idx & (nbq - 1)
                qo = pl.multiple_of(idx * HALF, HALF)
                ko = pl.multiple_of(idx * HALF + r * (2 * HALF), HALF)
                variant = jnp.where(jnp.logical_and(b == 0, has_prev == 0), 1,
                                    jnp.where(jnp.logical_and(b == nbq - 1, has_next == 0), 2, 0))
                work.append((r + (dil * HALF) * b, di * 3 + variant, qd[pl.ds(qo, HALF), :],
                             kd[pl.ds(ko, 3 * HALF), :], vd[pl.ds(ko, 3 * HALF), :]))
            def scores(batch):
                out = []
                for w in batch:
                    si = _dot_nt(w[3], _stack_heads(w[2])) + bias_ref[w[1]]
                    out.append((si, jnp.max(si, axis=0, keepdims=True)))
                return out

            def weighted_values(batch, sm):
                out = []
                for w, (si, mi) in zip(batch, sm):
                    pi = jnp.exp2(si - mi)
                    li = jnp.sum(pi, axis=0, keepdims=True)
                    hi, mid, lo = _split3(mi + jnp.log2(li))
                    terms = jnp.where(row16 == 0, hi.astype(F32),
                                      jnp.where(row16 == 1, mid.astype(F32),
                                                jnp.where(row16 == 2, lo.astype(F32), 0.0)))
                    lhs = jnp.concatenate([(pi * (1.0 / li)).astype(BF16), terms.astype(BF16)], axis=0)
                    rhs = jnp.concatenate([jnp.concatenate([w[4], zeros_v], axis=1), ones_rows], axis=0)
                    out.append(_dot_tn(lhs, rhs))
                return out

            def store(batch, outs):
                for w, oi in zip(batch, outs):
                    if dil == 1:
                        dst = pl.ds(pl.multiple_of(w[0], HALF), HALF)
                    else:
                        dst = pl.ds(w[0], HALF, stride=dil)
                    acc_o[di][dst, :] = jnp.where(m0_o, oi[0:HALF, 0:128], oi[HALF:2 * HALF, 0:128])
                    acc_m[di][dst, :] = jnp.where(m0_o, oi[0:HALF, 128:256], oi[HALF:2 * HALF, 128:256])

            batches = [work[j:j + BANDS_PER_BATCH] for j in range(0, BANDS_PER_STEP, BANDS_PER_BATCH)]
            sm = scores(batches[0])
            pending = None
            for j, batch in enumerate(batches):
                sm_next = scores(batches[j + 1]) if j + 1 < len(batches) else None
                outs = weighted_values(batch, sm)
                if pending is not None:
                    store(*pending)
                pending = (batch, outs)
                sm = sm_next
            store(*pending)
            return carry

        lax.fori_loop(0, SEQ_BLOCK // HALF // BANDS_PER_STEP, bands, 0)

    beta = beta_ref[...]

    def merge(t, carry):
        r0 = pl.multiple_of(t * rows, rows)
        sl = pl.ds(r0, rows)
        ms = [acc_m[d][sl, :] for d in range(3)]
        mx = jnp.maximum(jnp.maximum(ms[0], ms[1]), ms[2])
        num = jnp.zeros((rows, 128), F32)
        den = jnp.zeros((rows, 128), F32)
        for d in range(3):
            e = jnp.exp2(ms[d] - mx)
            num = num + e * acc_o[d][sl, :]
            den = den + e
        o_ref[sl, :] = ((num / den) * beta).astype(o_ref.dtype)
        return carry

    lax.fori_loop(0, SEQ_BLOCK // rows, merge, 0)


def _attn_bias_table():
    qi = np.arange(HALF)[:, None]
    kj = np.arange(3 * HALF)[None, :]
    rel = np.abs(kj - HALF - qi).astype(np.float32)
    n_heads = ATTN_WIDTH // HEAD_DIM
    slopes = 2.0 ** (-8.0 * (np.arange(n_heads, dtype=np.float32) + 1.0) / n_heads)
    keep = [rel <= HALF, (rel <= HALF) & (kj >= HALF), (rel <= HALF) & (kj < 2 * HALF)]
    tab = np.empty((N_PAIRS, 3, 3, 2, HALF, 3 * HALF), np.float32)
    for h in range(n_heads):
        for di, dil in enumerate(DILATIONS):
            for var in range(3):
                tab[h // 2, di, var, h % 2] = np.where(keep[var], -(slopes[h] * (dil * rel)) * LOG2_E, NEG_INF)
    return jnp.asarray(tab.reshape(N_PAIRS * 9, 2 * HALF, 3 * HALF).transpose(0, 2, 1))


def _attention(q, k, v, has_prev, has_next, attn_beta):
    ntok = q.shape[0]
    nb = ntok // SEQ_BLOCK
    hb = SEQ_BLOCK // HALO
    cur = lambda i, j, hp, hn: (i, j)
    prev = lambda i, j, hp, hn: (jnp.maximum(i * hb - 1, 0), j)
    nxt = lambda i, j, hp, hn: (jnp.minimum((i + 1) * hb, nb * hb - 1), j)
    blk = pl.BlockSpec((SEQ_BLOCK, 128), cur)
    halo_p = pl.BlockSpec((HALO, 128), prev)
    halo_n = pl.BlockSpec((HALO, 128), nxt)
    win = SEQ_BLOCK + 2 * HALO
    grid_spec = pltpu.PrefetchScalarGridSpec(
        num_scalar_prefetch=2,
        grid=(nb, N_PAIRS),
        in_specs=[blk, halo_p, blk, halo_n, halo_p, blk, halo_n,
                  pl.BlockSpec((1, 128), lambda i, j, hp, hn: (0, j)),
                  pl.BlockSpec((9, 3 * HALF, 2 * HALF), lambda i, j, hp, hn: (j, 0, 0))],
        out_specs=blk,
        scratch_shapes=[pltpu.VMEM((SEQ_BLOCK, 128), BF16), pltpu.VMEM((win, 128), BF16), pltpu.VMEM((win, 128), BF16),
                        pltpu.VMEM((SEQ_BLOCK, 128), F32), pltpu.VMEM((win, 128), F32), pltpu.VMEM((win, 128), F32),
                        [pltpu.VMEM((SEQ_BLOCK, 128), F32)] * 3,
                        [pltpu.VMEM((SEQ_BLOCK, 128), F32)] * 3],
    )
    return pl.pallas_call(
        _attn_kernel,
        grid_spec=grid_spec,
        out_shape=jax.ShapeDtypeStruct((ntok, ATTN_WIDTH), BF16),
        compiler_params=pltpu.CompilerParams(dimension_semantics=("arbitrary", "arbitrary"),
                                             vmem_limit_bytes=VMEM_LIMIT),
        name="attn",
    )(has_prev, has_next, q, k, k, k, v, v, v, attn_beta.reshape(1, -1), _attn_bias_table())


def _block_diag2(m):
    z = jnp.zeros_like(m[0])
    return jnp.concatenate([jnp.concatenate([m[0], z], axis=1), jnp.concatenate([z, m[1]], axis=1)], axis=0)


def _head_ones():
    idx = np.arange(RWKV_WIDTH) // HEAD_DIM
    return jnp.asarray((idx[:, None] == idx[None, :]).astype(np.float32), dtype=BF16)


def _stack_heads(x):
    m0 = _lane_is_head0(x.shape)
    z = jnp.zeros_like(x)
    return jnp.concatenate([jnp.where(m0, x, z), jnp.where(m0, z, x)], axis=0)


def _stack_heads_bf16(x):
    return _stack_heads(x.astype(BF16))


def _scan_masks(rev):
    c = CHUNK
    rowi = lax.broadcasted_iota(jnp.int32, (c, 128), 0)
    colj = lax.broadcasted_iota(jnp.int32, (c, 128), 1) & (c - 1)
    ti = lax.broadcasted_iota(jnp.int32, (c, c), 0)
    tj = lax.broadcasted_iota(jnp.int32, (c, c), 1)
    if rev:
        strict, incl, tri = colj > rowi, colj >= rowi, tj >= ti
    else:
        strict, incl, tri = colj < rowi, colj <= rowi, tj <= ti
    return dict(strict=strict, incl=incl, tri=tri.astype(BF16), eye2=(colj == rowi).astype(F32))


def _scan_pre(ins, masks):
    c = CHUNK
    idx = range(len(ins))
    lw = [a[0] for a in ins]
    v = [a[3] for a in ins]

    cl = []
    for i in idx:
        hi, mid, _ = _split3(lw[i])
        cl2 = _dot(masks[i]["tri"], jnp.concatenate([hi, mid], axis=1))
        cl.append(cl2[:, 0:128] + cl2[:, 128:256])
    yield
    g_tot = [jnp.exp(cl[i][0:1, :] if masks[i]["rev"] else cl[i][c - 1:c, :]) for i in idx]
    g_inv = [jnp.exp(-cl[i]) for i in idx]
    rt = [ins[i][1] * jnp.exp(cl[i]) for i in idx]
    bt = [ins[i][2] * jnp.exp(cl[i] - lw[i]) for i in idx]
    at = [-(ins[i][5] * g_inv[i]) for i in idx]
    kt = [ins[i][4] * g_inv[i] for i in idx]
    br = [jnp.concatenate([bt[i], rt[i]], axis=0).astype(BF16) for i in idx]
    lm = [_dot_nt(br[i], jnp.concatenate([_stack_heads_bf16(at[i]), _stack_heads_bf16(kt[i])], axis=0))
          for i in idx]
    akg = [jnp.concatenate([at[i] * g_tot[i], kt[i] * g_tot[i]], axis=0).astype(BF16) for i in idx]
    yield
    la = [jnp.where(masks[i]["strict"], lm[i][0:c, 0:128], 0.0) for i in idx]
    lk = [jnp.where(masks[i]["strict"], lm[i][0:c, 128:256], 0.0).astype(BF16) for i in idx]
    mam = [jnp.concatenate([jnp.where(masks[i]["incl"], lm[i][c:2 * c, 0:128], 0.0),
                            jnp.where(masks[i]["incl"], lm[i][c:2 * c, 128:256], 0.0)], axis=1).astype(BF16)
           for i in idx]

    sv = [_stack_heads_bf16(v[i]) for i in idx]
    lkv = [_dot(lk[i], sv[i]) for i in idx]

    t = [masks[i]["eye2"] + la[i] for i in idx]
    pw = [_dot(la[i].astype(BF16), _stack_heads_bf16(la[i])) for i in idx]
    yield
    for step in range(5):
        if step < 4:
            out = [_dot(pw[i].astype(BF16),
                        jnp.concatenate([_stack_heads_bf16(pw[i]), _stack_heads_bf16(t[i])], axis=1))
                   for i in idx]
            t = [t[i] + out[i][:, 128:256] for i in idx]
            pw = [out[i][:, 0:128] for i in idx]
        else:
            t = [t[i] + _dot(pw[i].astype(BF16), _stack_heads_bf16(t[i])) for i in idx]
        yield
    return [dict(br=br[i], lkv=lkv[i], mam=mam[i], sv=sv[i], t=t[i].astype(BF16), v=v[i].astype(BF16),
                 akg=akg[i], g_tot=g_tot[i]) for i in idx]


def _scan_seq(pre, zs, n_chunks):
    c = CHUNK
    per = len(zs)
    idx = range(per)
    zr = lax.broadcasted_iota(jnp.int32, (128, 128), 0) < HEAD_DIM
    zc = lax.broadcasted_iota(jnp.int32, (128, 128), 1) < HEAD_DIM
    same_head = zr == zc
    ys = []
    for k in range(n_chunks):
        p = pre[k * per:(k + 1) * per]
        brz = [_dot_nt(p[i]["br"], zs[i].astype(BF16)) for i in idx]
        yield
        u = [_dot(p[i]["t"], _stack_heads_bf16(brz[i][0:c] + p[i]["lkv"])) for i in idx]
        yield
        ys += [brz[i][c:2 * c] + _dot(p[i]["mam"], jnp.concatenate([_stack_heads_bf16(u[i]), p[i]["sv"]], axis=0))
               for i in idx]
        zs = [zs[i] * p[i]["g_tot"]
              + jnp.where(same_head, _dot_tn(jnp.concatenate([u[i].astype(BF16), p[i]["v"]], axis=0), p[i]["akg"]), 0.0)
              for i in idx]
        yield
    return ys, zs


def _run(gen):
    try:
        while True:
            next(gen)
    except StopIteration as done:
        return done.value


def _interleave(gen_a, gen_b):
    live = {"a": gen_a, "b": gen_b}
    result = {}
    while live:
        for name in list(live):
            try:
                next(live[name])
            except StopIteration as done:
                result[name] = done.value
                del live[name]
    return result["a"], result["b"]


def _scan_kernel(bmap_ref, reset_ref,
                 rf, kkf, vf, lwf, kdf, bdf, rb, kkb, vb, lwb, kdb, bdb,
                 yf_ref, yb_ref, zf, zb):
    del bmap_ref
    g = pl.program_id(0)

    @pl.when(reset_ref[g] == 1)
    def _():
        zf[...] = jnp.zeros_like(zf)
        zb[...] = jnp.zeros_like(zb)

    nchunk = SCAN_BLOCK // CHUNK
    n_groups = nchunk // CHUNKS_PER_GROUP
    mask_f = dict(_scan_masks(False), rev=False)
    mask_b = dict(_scan_masks(True), rev=True)
    dirs = ((mask_f, (lwf, rf, kkf, vf, kdf, bdf), yf_ref, zf),
            (mask_b, (lwb, rb, kkb, vb, kdb, bdb), yb_ref, zb))

    def group_items(gi):
        ins, masks, outs = [], [], []
        for k in range(CHUNKS_PER_GROUP):
            step = gi * CHUNKS_PER_GROUP + k
            for mask, refs, y_ref, _ in dirs:
                cc = (nchunk - 1 - step) if mask["rev"] else step
                rows = slice(cc * CHUNK, (cc + 1) * CHUNK)
                for pr in range(N_PAIRS):
                    lanes = slice(128 * pr, 128 * (pr + 1))
                    ins.append([ref[rows, lanes] for ref in refs])
                    masks.append(mask)
                    outs.append((y_ref, rows, lanes))
        return ins, masks, outs

    zs = [z_ref[pr] for _, _, _, z_ref in dirs for pr in range(N_PAIRS)]
    ins, masks, outs = group_items(0)
    pre = _run(_scan_pre(ins, masks))
    for gi in range(n_groups):
        seq = _scan_seq(pre, zs, CHUNKS_PER_GROUP)
        if gi + 1 < n_groups:
            ins, masks, next_outs = group_items(gi + 1)
            pre, (ys, zs) = _interleave(_scan_pre(ins, masks), seq)
        else:
            ys, zs = _run(seq)
        for (y_ref, rows, lanes), y in zip(outs, ys):
            y_ref[rows, lanes] = y
        if gi + 1 < n_groups:
            outs = next_outs
    state_refs = [(z_ref, pr) for _, _, _, z_ref in dirs for pr in range(N_PAIRS)]
    for (z_ref, pr), z in zip(state_refs, zs):
        z_ref[pr] = z


def _scan(r, kk, v, lw, kd, bd, bmap_bwd, reset):
    ntok = r.shape[0]
    c = RWKV_WIDTH
    ns = ntok // SCAN_BLOCK
    fwd = lambda g, bm, rs: (g, 0)
    bwd = lambda g, bm, rs: (bm[g], 0)
    fwd2 = lambda g, bm, rs: (0, g, 0)
    bwd2 = lambda g, bm, rs: (1, bm[g], 0)
    tokf = pl.BlockSpec((SCAN_BLOCK, c), fwd)
    tokb = pl.BlockSpec((SCAN_BLOCK, c), bwd)
    dirf = pl.BlockSpec((None, SCAN_BLOCK, c), fwd2)
    dirb = pl.BlockSpec((None, SCAN_BLOCK, c), bwd2)
    grid_spec = pltpu.PrefetchScalarGridSpec(
        num_scalar_prefetch=2,
        grid=(ns,),
        in_specs=[tokf, tokf, tokf, dirf, dirf, dirf, tokb, tokb, tokb, dirb, dirb, dirb],
        out_specs=[tokf, tokb],
        scratch_shapes=[pltpu.VMEM((N_PAIRS, 128, 128), F32), pltpu.VMEM((N_PAIRS, 128, 128), F32)],
    )
    tok = jax.ShapeDtypeStruct((ntok, c), F32)
    return pl.pallas_call(
        _scan_kernel,
        grid_spec=grid_spec,
        out_shape=[tok, tok],
        compiler_params=pltpu.CompilerParams(dimension_semantics=("arbitrary",), vmem_limit_bytes=VMEM_LIMIT),
        name="rwkv_scan",
    )(bmap_bwd, reset, r, kk, v, lw, kd, bd, r, kk, v, lw, kd, bd)


def _post_ffn_kernel(n_prompt_tiles, mrow_ref, xp_ref, xs_ref, yf_ref, yb_ref, bonus_ref, g_ref, attn_ref,
                     gt1_ref, sh2_ref, sc2_ref, gt2_ref, g2_ref, lnw_ref, lnb_ref, ones_ref, wo_ref, w1_ref, w2_ref,
                     op_ref, os_ref):
    del mrow_ref
    is_prompt = pl.program_id(0) < n_prompt_tiles
    x = jnp.where(is_prompt, xp_ref[...], xs_ref[...])

    y = yf_ref[...] + yb_ref[...]
    ones_bd = ones_ref[...]
    mu = _dot(y.astype(BF16), ones_bd) * (1.0 / HEAD_DIM)
    d = y - mu
    var = _dot((d * d).astype(BF16), ones_bd) * (1.0 / HEAD_DIM)
    yn = d * lax.rsqrt(var + LN_X_EPS) * lnw_ref[...] + lnb_ref[...]
    rw = ((yn + bonus_ref[...]) * g_ref[...]).astype(BF16)

    a = ATTN_WIDTH
    mix = _dot(attn_ref[...], wo_ref[0:a, :]) + _dot(rw, wo_ref[a:, :])
    x1 = x + gt1_ref[0] * mix
    h = _modulated_rmsnorm(x1, g2_ref[...], sc2_ref[0], sh2_ref[0]).astype(BF16)

    acc = jnp.zeros((h.shape[0], D_MODEL), F32)
    for j in range(D_FF // FF_STEP):
        cols = slice(j * FF_STEP, (j + 1) * FF_STEP)
        act = jnp.maximum(_dot(h, w1_ref[:, cols]), 0.0)
        acc = acc + _dot((act * act).astype(BF16), w2_ref[cols, :])
    out = x1 + gt2_ref[0] * acc

    @pl.when(is_prompt)
    def _():
        op_ref[...] = out

    @pl.when(jnp.logical_not(is_prompt))
    def _():
        os_ref[...] = out


def _post_ffn(xp, xs, yf, yb, bonus, g, attn, mod3, mrow_tile, g_norm2, ln_x_w, ln_x_b, w_out_bf, w1_bf, w2_bf):
    n_prompt_tiles = xp.shape[0] // ROW_TILE
    nt = (xp.shape[0] + xs.shape[0]) // ROW_TILE
    c = RWKV_WIDTH
    row = lambda i, mr: (i, 0)
    const = lambda i, mr: (0, 0)
    resident = lambda shape: pl.BlockSpec(shape, const, pipeline_mode=pl.Buffered(1))
    modspec = lambda j: pl.BlockSpec((1, 1, D_MODEL), lambda i, mr: (mr[i] * 6 + j, 0, 0))
    xp_spec, xs_spec = _two_group_specs(n_prompt_tiles)
    half = pl.BlockSpec((ROW_TILE, c), row)
    grid_spec = pltpu.PrefetchScalarGridSpec(
        num_scalar_prefetch=1,
        grid=(nt,),
        in_specs=[xp_spec, xs_spec, half, half, half, half, half,
                  modspec(2), modspec(3), modspec(4), modspec(5),
                  pl.BlockSpec((1, D_MODEL), const), pl.BlockSpec((1, c), const), pl.BlockSpec((1, c), const),
                  resident((c, c)), resident((D_MODEL, D_MODEL)), resident((D_MODEL, D_FF)),
                  resident((D_FF, D_MODEL))],
        out_specs=[xp_spec, xs_spec],
    )
    return pl.pallas_call(
        functools.partial(_post_ffn_kernel, n_prompt_tiles),
        grid_spec=grid_spec,
        out_shape=[jax.ShapeDtypeStruct(xp.shape, F32), jax.ShapeDtypeStruct(xs.shape, F32)],
        compiler_params=pltpu.CompilerParams(dimension_semantics=("arbitrary",), vmem_limit_bytes=VMEM_LIMIT),
        name="post_ffn",
    )(mrow_tile, xp, xs, yf, yb, bonus, g, attn, mod3, mod3, mod3, mod3, g_norm2.reshape(1, -1),
      ln_x_w.reshape(1, -1), ln_x_b.reshape(1, -1), _head_ones(), w_out_bf, w1_bf, w2_bf)


def _sequence_tables(seq_blocks):
    mod_row, has_prev, has_next = [], [], []
    for s, n in enumerate(seq_blocks):
        for b in range(n):
            mod_row.append(s)
            has_prev.append(int(b > 0))
            has_next.append(int(b < n - 1))
    per = SEQ_BLOCK // SCAN_BLOCK
    bmap_bwd, reset = [], []
    start = 0
    for n in seq_blocks:
        ns = n * per
        for t in range(ns):
            bmap_bwd.append(start + ns - 1 - t)
            reset.append(int(t == 0))
        start += ns
    i32 = lambda a: jnp.asarray(np.asarray(a, np.int32))
    rep = lambda a, k: np.repeat(np.asarray(a, np.int32), k)
    row_tiles = SEQ_BLOCK // ROW_TILE
    first = np.zeros(len(mod_row) * row_tiles, np.int32)
    last = np.zeros(len(mod_row) * row_tiles, np.int32)
    for b in range(len(mod_row)):
        if not has_prev[b]:
            first[b * row_tiles] = 1
        if not has_next[b]:
            last[(b + 1) * row_tiles - 1] = 1
    return dict(mrow_tile=i32(rep(mod_row, row_tiles)), has_prev=i32(has_prev), has_next=i32(has_next),
                bmap_bwd=i32(bmap_bwd), reset=i32(reset), first=i32(first), last=i32(last))


def _layer(xp, xs, c_all, tabs, w_ada, b_ada, g_norm1, g_norm2, w_in, q_norm_g, k_norm_g, attn_beta,
           mu_prev, mu_next, w0, w_up, a0, a_up, g_up, k_k, k_a, r_k, ln_x_w, ln_x_b, w_out, w_ff1, w_ff2):
    mod = _ada(c_all, w_ada, b_ada)
    mod3 = mod.reshape(-1, 1, D_MODEL)
    q, k, v, r, kk, vv, g, bonus, lw, kd, bd = _inproj(
        xp, xs, mod3, tabs, g_norm1, w_in.astype(BF16), q_norm_g, k_norm_g,
        mu_prev, mu_next, w0, w_up, a0, a_up, g_up, k_k, k_a, r_k)
    attn = _attention(q, k, v, tabs["has_prev"], tabs["has_next"], attn_beta)
    yf, yb = _scan(r, kk, vv, lw, kd, bd, tabs["bmap_bwd"], tabs["reset"])
    return _post_ffn(xp, xs, yf, yb, bonus, g, attn, mod3, tabs["mrow_tile"], g_norm2, ln_x_w, ln_x_b,
                     w_out.astype(BF16), w_ff1.astype(BF16), w_ff2.astype(BF16))


def kernel(x_prompt, x_sample, c_prompt, c_sample, w_ada, b_ada, g_norm1, g_norm2, w_in, q_norm_g, k_norm_g, attn_beta, mu_prev, mu_next, w0, w_up, a0, a_up, g_up, k_k, k_a, r_k, ln_x_w, ln_x_b, w_out, w_ff1, w_ff2):
    bp, sp, d = x_prompt.shape
    bs, ss, _ = x_sample.shape
    assert d == D_MODEL and sp % SEQ_BLOCK == 0 and ss % SEQ_BLOCK == 0
    seq_blocks = [sp // SEQ_BLOCK] * bp + [ss // SEQ_BLOCK] * bs
    tabs = _sequence_tables(seq_blocks)
    n_seq = bp + bs
    pad = (-n_seq) % 8
    c_all = jnp.concatenate([c_prompt, c_sample, jnp.zeros((pad, d), F32)], axis=0)
    yp = x_prompt.reshape(bp * sp, d)
    ys = x_sample.reshape(bs * ss, d)
    for i in range(w_ada.shape[0]):
        yp, ys = _layer(yp, ys, c_all, tabs, w_ada[i], b_ada[i], g_norm1[i], g_norm2[i], w_in[i], q_norm_g[i],
                        k_norm_g[i], attn_beta[i], mu_prev[i], mu_next[i], w0[i], w_up[i], a0[i], a_up[i], g_up[i],
                        k_k[i], k_a[i], r_k[i], ln_x_w[i], ln_x_b[i], w_out[i], w_ff1[i], w_ff2[i])
    return (yp.reshape(bp, sp, d), ys.reshape(bs, ss, d))
```

```python
import functools

import numpy as np
import jax
import jax.numpy as jnp
from jax import lax
from jax.experimental import pallas as pl
from jax.experimental.pallas import tpu as pltpu

F32 = jnp.float32
BF16 = jnp.bfloat16

D_MODEL = 1024
HEAD_DIM = 64
ATTN_WIDTH = 512
RWKV_WIDTH = 512
N_PAIRS = ATTN_WIDTH // 128
RWKV_IN = 1920
IN_WIDTH = 3 * ATTN_WIDTH + RWKV_IN
D_FF = 4096
DILATIONS = (1, 4, 16)
HALF = 64
NORM_EPS = 1e-6
LN_X_EPS = 64e-5
NEG_INF = -1e30
LOG2_E = 1.4426950408889634

SEQ_BLOCK = 2048
HALO = 1024
SCAN_BLOCK = 1024
CHUNK = 64
CHUNKS_PER_GROUP = 2
ROW_TILE = 512
BANDS_PER_BATCH = 8
BANDS_PER_STEP = 32
FF_STEP = 1024

VMEM_LIMIT = 56 * 1024 * 1024


def _dot(a, b):
    return jnp.dot(a, b, preferred_element_type=F32)


def _dot_nt(a, b):
    return lax.dot_general(a, b, (((1,), (1,)), ((), ())), preferred_element_type=F32)


def _dot_tn(a, b):
    return lax.dot_general(a, b, (((0,), (0,)), ((), ())), preferred_element_type=F32)


def _split3(x):
    hi = x.astype(BF16)
    r1 = x - hi.astype(F32)
    mid = r1.astype(BF16)
    lo = (r1 - mid.astype(F32)).astype(BF16)
    return hi, mid, lo


def _sigmoid(x):
    return 1.0 / (1.0 + jnp.exp(-x))


def _ada_kernel(c_ref, w_ref, b_ref, o_ref):
    c = c_ref[...]
    s = c * _sigmoid(c)
    s1, s2, _ = _split3(s)
    w = w_ref[...]
    w1, w2, _ = _split3(w)
    o_ref[...] = _dot(s1, w1) + (_dot(s1, w2) + _dot(s2, w1)) + b_ref[...]


def _ada(c_all, w_ada, b_ada):
    n = c_all.shape[0]
    nt = w_ada.shape[1] // D_MODEL
    return pl.pallas_call(
        _ada_kernel,
        grid=(nt,),
        in_specs=[pl.BlockSpec((n, D_MODEL), lambda j: (0, 0)),
                  pl.BlockSpec((D_MODEL, D_MODEL), lambda j: (0, j)),
                  pl.BlockSpec((1, D_MODEL), lambda j: (0, j))],
        out_specs=pl.BlockSpec((n, D_MODEL), lambda j: (0, j)),
        out_shape=jax.ShapeDtypeStruct((n, w_ada.shape[1]), F32),
        compiler_params=pltpu.CompilerParams(dimension_semantics=("arbitrary",), vmem_limit_bytes=VMEM_LIMIT),
        name="ada",
    )(c_all, w_ada, b_ada.reshape(1, -1))


def _modulated_rmsnorm(x, g, scale, shift):
    ms = jnp.mean(x * x, axis=-1, keepdims=True)
    return (x * lax.rsqrt(ms + NORM_EPS) * g) * (1.0 + scale) + shift


def _two_group_specs(n_prompt_tiles):
    def prompt(i, *_):
        return (jnp.minimum(i, n_prompt_tiles - 1), 0)

    def sample(i, *_):
        return (jnp.maximum(i - n_prompt_tiles, 0), 0)
    return pl.BlockSpec((ROW_TILE, D_MODEL), prompt), pl.BlockSpec((ROW_TILE, D_MODEL), sample)


def _head_rms_scale(z, ones_bd):
    ssq = _dot((z * z).astype(BF16), ones_bd)
    return lax.rsqrt(ssq * (1.0 / HEAD_DIM) + NORM_EPS)


def _inproj_kernel(n_prompt_tiles, mrow_ref, first_ref, last_ref,
                   xp_ref, xs_ref, xpp_ref, xpn_ref, xsp_ref, xsn_ref,
                   g_ref, sh_ref, sc_ref, w_ref, qg_ref, kg_ref, ones_ref,
                   mup_ref, mun_ref, w0_ref, wup_ref, a0_ref, aup_ref, gup_ref, kk_ref, ka_ref, rk_ref,
                   q_ref, k_ref, v_ref,
                   r_out, kk_out, v_out, g_out, bonus_out, lw_out, kd_out, bd_out):
    del mrow_ref
    i = pl.program_id(0)
    is_prompt = i < n_prompt_tiles
    g1, sc, sh = g_ref[...], sc_ref[0], sh_ref[0]
    keep_prev = jnp.where(first_ref[i] == 0, 1.0, 0.0)
    keep_next = jnp.where(last_ref[i] == 0, 1.0, 0.0)
    h = _modulated_rmsnorm(jnp.where(is_prompt, xp_ref[...], xs_ref[...]), g1, sc, sh)
    h_prev = _modulated_rmsnorm(jnp.where(is_prompt, xpp_ref[...], xsp_ref[...]), g1, sc, sh) * keep_prev
    h_next = _modulated_rmsnorm(jnp.where(is_prompt, xpn_ref[...], xsn_ref[...]), g1, sc, sh) * keep_next

    a = ATTN_WIDTH
    hb = h.astype(BF16)
    ones_bd = ones_ref[...]
    q = _dot(hb, w_ref[:, 0:a])
    q_ref[...] = q * _head_rms_scale(q, ones_bd) * (qg_ref[...] * (HEAD_DIM ** -0.5 * LOG2_E))
    k = _dot(hb, w_ref[:, a:2 * a])
    k_ref[...] = k * _head_rms_scale(k, ones_bd) * kg_ref[...]
    v_ref[...] = _dot(hb, w_ref[:, 2 * a:3 * a])

    tm = h.shape[0]
    hext = jnp.concatenate([h_prev, h, h_next], axis=0).astype(BF16)
    next_amt = hext.shape[0] - 1

    def shifted(lo, hi):
        p = _dot(hext, w_ref[:, 3 * a + lo:3 * a + hi])
        prev = pltpu.roll(p, 1, 0)
        nxt = pltpu.roll(p, next_amt, 0)
        ps = p + mup_ref[:, lo:hi] * (prev - p) + mun_ref[:, lo:hi] * (nxt - p)
        return ps[8:8 + tm]

    c = RWKV_WIDTH
    r = shifted(0, c)
    k = shifted(c, 2 * c)
    v = shifted(2 * c, 3 * c)
    low = shifted(3 * c, 3 * c + 384)
    wd = low[:, 0:128]
    ad = low[:, 128:256]
    gd = low[:, 256:384]

    w_raw = w0_ref[...] + _dot(jnp.tanh(wd).astype(BF16), wup_ref[...])
    lw = (-float(np.exp(-0.5))) * _sigmoid(w_raw)
    a_rate = _sigmoid(a0_ref[...] + _dot(ad.astype(BF16), aup_ref[...]))
    gate = _dot(_sigmoid(gd).astype(BF16), gup_ref[...])

    kkv = k * kk_ref[...]
    ssq = _dot((kkv * kkv).astype(BF16), ones_bd)
    kkv = kkv * lax.rsqrt(jnp.maximum(ssq, 1e-24))
    ka = ka_ref[...]
    kd0 = k * (1.0 + (a_rate[:, 0:c] - 1.0) * ka)
    kd1 = k * (1.0 + (a_rate[:, c:2 * c] - 1.0) * ka)
    hsum = _dot((r * (kd0 + kd1) * rk_ref[...]).astype(BF16), ones_bd)

    r_out[...] = r.astype(BF16)
    kk_out[...] = kkv.astype(BF16)
    v_out[...] = v.astype(BF16)
    g_out[...] = gate
    bonus_out[...] = hsum * v
    lw_out[0] = lw[:, 0:c]
    lw_out[1] = lw[:, c:2 * c]
    kd_out[0] = kd0.astype(BF16)
    kd_out[1] = kd1.astype(BF16)
    bd_out[0] = (kkv * a_rate[:, 0:c]).astype(BF16)
    bd_out[1] = (kkv * a_rate[:, c:2 * c]).astype(BF16)


def _inproj(xp, xs, mod3, tabs, g_norm1, w_in_bf, q_norm_g, k_norm_g,
            mu_prev, mu_next, w0, w_up, a0, a_up, g_up, k_k, k_a, r_k):
    n_prompt_tiles = xp.shape[0] // ROW_TILE
    ntok = xp.shape[0] + xs.shape[0]
    nt = ntok // ROW_TILE
    c = RWKV_WIDTH
    per8 = ROW_TILE // 8
    row = lambda i, *_: (i, 0)
    const = lambda i, *_: (0, 0)
    resident = lambda shape: pl.BlockSpec(shape, const, pipeline_mode=pl.Buffered(1))
    xp_spec, xs_spec = _two_group_specs(n_prompt_tiles)

    def halo(n_rows, tile_of, side):
        def index(i, *_):
            t = tile_of(i)
            if side < 0:
                return (jnp.maximum(t * per8 - 1, 0), 0)
            return (jnp.minimum((t + 1) * per8, n_rows // 8 - 1), 0)
        return pl.BlockSpec((8, D_MODEL), index)

    prompt_tile = lambda i: jnp.minimum(i, n_prompt_tiles - 1)
    sample_tile = lambda i: jnp.maximum(i - n_prompt_tiles, 0)
    n_heads = ATTN_WIDTH // HEAD_DIM
    per_head = lambda g: jnp.tile(g.reshape(1, HEAD_DIM), (1, n_heads))
    vec = lambda n: pl.BlockSpec((1, n), const)
    grid_spec = pltpu.PrefetchScalarGridSpec(
        num_scalar_prefetch=3,
        grid=(nt,),
        in_specs=[xp_spec, xs_spec,
                  halo(xp.shape[0], prompt_tile, -1), halo(xp.shape[0], prompt_tile, +1),
                  halo(xs.shape[0], sample_tile, -1), halo(xs.shape[0], sample_tile, +1),
                  vec(D_MODEL),
                  pl.BlockSpec((1, 1, D_MODEL), lambda i, mr, *_: (mr[i] * 6 + 0, 0, 0)),
                  pl.BlockSpec((1, 1, D_MODEL), lambda i, mr, *_: (mr[i] * 6 + 1, 0, 0)),
                  resident((D_MODEL, IN_WIDTH)), vec(ATTN_WIDTH), vec(ATTN_WIDTH), resident((ATTN_WIDTH, ATTN_WIDTH)),
                  vec(RWKV_IN), vec(RWKV_IN), vec(2 * c), resident((128, 2 * c)), vec(2 * c), resident((128, 2 * c)),
                  resident((128, c)), vec(c), vec(c), vec(c)],
        out_specs=[pl.BlockSpec((ROW_TILE, ATTN_WIDTH), row)] * 3 + [pl.BlockSpec((ROW_TILE, c), row)] * 5
                  + [pl.BlockSpec((2, ROW_TILE, c), lambda i, *_: (0, i, 0))] * 3,
    )
    tok = lambda dt: jax.ShapeDtypeStruct((ntok, c), dt)
    tok2 = lambda dt: jax.ShapeDtypeStruct((2, ntok, c), dt)
    return pl.pallas_call(
        functools.partial(_inproj_kernel, n_prompt_tiles),
        grid_spec=grid_spec,
        out_shape=[jax.ShapeDtypeStruct((ntok, ATTN_WIDTH), F32)] * 3
                  + [tok(BF16), tok(BF16), tok(BF16), tok(F32), tok(F32), tok2(F32), tok2(BF16), tok2(BF16)],
        compiler_params=pltpu.CompilerParams(dimension_semantics=("arbitrary",), vmem_limit_bytes=VMEM_LIMIT),
        name="inproj",
    )(tabs["mrow_tile"], tabs["first"], tabs["last"], xp, xs, xp, xp, xs, xs,
      g_norm1.reshape(1, -1), mod3, mod3, w_in_bf, per_head(q_norm_g), per_head(k_norm_g), _head_ones(),
      mu_prev.reshape(1, -1), mu_next.reshape(1, -1), w0.reshape(1, -1), _block_diag2(w_up).astype(BF16),
      a0.reshape(1, -1), _block_diag2(a_up).astype(BF16), g_up.astype(BF16),
      k_k.reshape(1, -1), k_a.reshape(1, -1), r_k.reshape(1, -1))


def _lane_is_head0(shape):
    return lax.broadcasted_iota(jnp.int32, shape, len(shape) - 1) < HEAD_DIM


def _attn_kernel(hasprev_ref, hasnext_ref,
                 q_ref, kp_ref, kc_ref, kx_ref, vp_ref, vc_ref, vx_ref, beta_ref, bias_ref,
                 o_ref,
                 qd, kd, vd, tq, tk, tv, acc_o, acc_m):
    i = pl.program_id(0)
    has_prev = hasprev_ref[i]
    has_next = hasnext_ref[i]
    rows = 256

    m0_o = _lane_is_head0((HALF, 128))
    row16 = lax.broadcasted_iota(jnp.int32, (16, 128), 0)
    ones3 = jnp.where(row16 < 3, 1.0, 0.0).astype(BF16)
    ones_rows = jnp.concatenate([jnp.zeros((16, 128), BF16), ones3], axis=1)
    zeros_v = jnp.zeros((3 * HALF, 128), BF16)

    def cast_rows(src_ref, src_off, dst_ref, dst_off, nrows, chunk):
        def body(t, carry):
            r0 = pl.multiple_of(t * chunk, chunk)
            dst_ref[pl.ds(dst_off + r0, chunk), :] = src_ref[pl.ds(src_off + r0, chunk), :].astype(BF16)
            return carry
        lax.fori_loop(0, nrows // chunk, body, 0)

    wq, wk = SEQ_BLOCK // 4, (SEQ_BLOCK + 2 * HALO) // 4

    def split4(r0, carry):
        tq[pl.ds(pl.multiple_of(r0 * wq, HALF), wq), :] = q_ref[pl.ds(r0, wq, stride=4), :]
        base = pl.multiple_of(r0 * wk, HALF)
        for prev_ref, cur_ref, next_ref, dst in ((kp_ref, kc_ref, kx_ref, tk), (vp_ref, vc_ref, vx_ref, tv)):
            dst[pl.ds(base, HALO // 4), :] = prev_ref[pl.ds(r0, HALO // 4, stride=4), :]
            dst[pl.ds(base + HALO // 4, wq), :] = cur_ref[pl.ds(r0, wq, stride=4), :]
            dst[pl.ds(base + HALO // 4 + wq, HALO // 4), :] = next_ref[pl.ds(r0, HALO // 4, stride=4), :]
        return carry

    for di, dil in enumerate(DILATIONS):
        lq = SEQ_BLOCK // dil
        lk = lq + 2 * HALF
        nbq = lq // HALF
        pbase = HALO - HALF * dil

        if dil == 1:
            cast_rows(q_ref, 0, qd, 0, lq, rows)
            for prev_ref, cur_ref, next_ref, dst in ((kp_ref, kc_ref, kx_ref, kd), (vp_ref, vc_ref, vx_ref, vd)):
                cast_rows(prev_ref, pbase, dst, 0, HALF, HALF)
                cast_rows(cur_ref, 0, dst, HALF, lq, rows)
                cast_rows(next_ref, 0, dst, HALF + lq, HALF, HALF)
        elif dil == 4:
            assert lq == wq and pbase % 4 == 0
            lax.fori_loop(0, 4, split4, 0)
            cast_rows(tq, 0, qd, 0, SEQ_BLOCK, rows)
            for r0 in range(4):
                cast_rows(tk, r0 * wk + pbase // 4, kd, r0 * lk, lk, 2 * HALF)
                cast_rows(tv, r0 * wk + pbase // 4, vd, r0 * lk, lk, 2 * HALF)
        elif dil == 16:
            assert pbase == 0 and lk * dil == SEQ_BLOCK + 2 * HALO

            def regroup16(r, carry, lq=lq, lk=lk, wq=wq, wk=wk):
                r0 = r & 3
                r1 = r >> 2
                qd[pl.ds(pl.multiple_of(r * lq, HALF), lq), :] = tq[pl.ds(r0 * wq + r1, lq, stride=4), :].astype(BF16)
                ko = pl.multiple_of(r * lk, HALF)
                kd[pl.ds(ko, lk), :] = tk[pl.ds(r0 * wk + r1, lk, stride=4), :].astype(BF16)
                vd[pl.ds(ko, lk), :] = tv[pl.ds(r0 * wk + r1, lk, stride=4), :].astype(BF16)
                return carry
            lax.fori_loop(0, dil, regroup16, 0)
        else:
            raise NotImplementedError(f"dilation {dil}")

        shift = nbq.bit_length() - 1

        def bands(it, carry, di=di, dil=dil, nbq=nbq, shift=shift):
            work = []
            for u in range(BANDS_PER_STEP):
                idx = it * BANDS_PER_STEP + u
                r = idx >> shift
                b = idx & (nbq - 1)
                qo = pl.multiple_of(idx * HALF, HALF)
                ko = pl.multiple_of(idx * HALF + r * (2 * HALF), HALF)
                variant = jnp.where(jnp.logical_and(b == 0, has_prev == 0), 1,
                                    jnp.where(jnp.logical_and(b == nbq - 1, has_next == 0), 2, 0))
                work.append((r + (dil * HALF) * b, di * 3 + variant, qd[pl.ds(qo, HALF), :],
                             kd[pl.ds(ko, 3 * HALF), :], vd[pl.ds(ko, 3 * HALF), :]))
            def scores(batch):
                out = []
                for w in batch:
                    si = _dot_nt(w[3], _stack_heads(w[2])) + bias_ref[w[1]]
                    out.append((si, jnp.max(si, axis=0, keepdims=True)))
                return out

            def weighted_values(batch, sm):
                out = []
                for w, (si, mi) in zip(batch, sm):
                    pi = jnp.exp2(si - mi)
                    li = jnp.sum(pi, axis=0, keepdims=True)
                    hi, mid, lo = _split3(mi + jnp.log2(li))
                    terms = jnp.where(row16 == 0, hi.astype(F32),
                                      jnp.where(row16 == 1, mid.astype(F32),
                                                jnp.where(row16 == 2, lo.astype(F32), 0.0)))
                    lhs = jnp.concatenate([(pi * (1.0 / li)).astype(BF16), terms.astype(BF16)], axis=0)
                    rhs = jnp.concatenate([jnp.concatenate([w[4], zeros_v], axis=1), ones_rows], axis=0)
                    out.append(_dot_tn(lhs, rhs))
                return out

            def store(batch, outs):
                for w, oi in zip(batch, outs):
                    if dil == 1:
                        dst = pl.ds(pl.multiple_of(w[0], HALF), HALF)
                    else:
                        dst = pl.ds(w[0], HALF, stride=dil)
                    acc_o[di][dst, :] = jnp.where(m0_o, oi[0:HALF, 0:128], oi[HALF:2 * HALF, 0:128])
                    acc_m[di][dst, :] = jnp.where(m0_o, oi[0:HALF, 128:256], oi[HALF:2 * HALF, 128:256])

            batches = [work[j:j + BANDS_PER_BATCH] for j in range(0, BANDS_PER_STEP, BANDS_PER_BATCH)]
            sm = scores(batches[0])
            pending = None
            for j, batch in enumerate(batches):
                sm_next = scores(batches[j + 1]) if j + 1 < len(batches) else None
                outs = weighted_values(batch, sm)
                if pending is not None:
                    store(*pending)
                pending = (batch, outs)
                sm = sm_next
            store(*pending)
            return carry

        lax.fori_loop(0, SEQ_BLOCK // HALF // BANDS_PER_STEP, bands, 0)

    beta = beta_ref[...]

    def merge(t, carry):
        r0 = pl.multiple_of(t * rows, rows)
        sl = pl.ds(r0, rows)
        ms = [acc_m[d][sl, :] for d in range(3)]
        mx = jnp.maximum(jnp.maximum(ms[0], ms[1]), ms[2])
        num = jnp.zeros((rows, 128), F32)
        den = jnp.zeros((rows, 128), F32)
        for d in range(3):
            e = jnp.exp2(ms[d] - mx)
            num = num + e * acc_o[d][sl, :]
            den = den + e
        o_ref[sl, :] = ((num / den) * beta).astype(o_ref.dtype)
        return carry

    lax.fori_loop(0, SEQ_BLOCK // rows, merge, 0)


def _attn_bias_table():
    qi = np.arange(HALF)[:, None]
    kj = np.arange(3 * HALF)[None, :]
    rel = np.abs(kj - HALF - qi).astype(np.float32)
    n_heads = ATTN_WIDTH // HEAD_DIM
    slopes = 2.0 ** (-8.0 * (np.arange(n_heads, dtype=np.float32) + 1.0) / n_heads)
    keep = [rel <= HALF, (rel <= HALF) & (kj >= HALF), (rel <= HALF) & (kj < 2 * HALF)]
    tab = np.empty((N_PAIRS, 3, 3, 2, HALF, 3 * HALF), np.float32)
    for h in range(n_heads):
        for di, dil in enumerate(DILATIONS):
            for var in range(3):
                tab[h // 2, di, var, h % 2] = np.where(keep[var], -(slopes[h] * (dil * rel)) * LOG2_E, NEG_INF)
    return jnp.asarray(tab.reshape(N_PAIRS * 9, 2 * HALF, 3 * HALF).transpose(0, 2, 1))


def _attention(q, k, v, has_prev, has_next, attn_beta):
    ntok = q.shape[0]
    nb = ntok // SEQ_BLOCK
    hb = SEQ_BLOCK // HALO
    cur = lambda i, j, hp, hn: (i, j)
    prev = lambda i, j, hp, hn: (jnp.maximum(i * hb - 1, 0), j)
    nxt = lambda i, j, hp, hn: (jnp.minimum((i + 1) * hb, nb * hb - 1), j)
    blk = pl.BlockSpec((SEQ_BLOCK, 128), cur)
    halo_p = pl.BlockSpec((HALO, 128), prev)
    halo_n = pl.BlockSpec((HALO, 128), nxt)
    win = SEQ_BLOCK + 2 * HALO
    grid_spec = pltpu.PrefetchScalarGridSpec(
        num_scalar_prefetch=2,
        grid=(nb, N_PAIRS),
        in_specs=[blk, halo_p, blk, halo_n, halo_p, blk, halo_n,
                  pl.BlockSpec((1, 128), lambda i, j, hp, hn: (0, j)),
                  pl.BlockSpec((9, 3 * HALF, 2 * HALF), lambda i, j, hp, hn: (j, 0, 0))],
        out_specs=blk,
        scratch_shapes=[pltpu.VMEM((SEQ_BLOCK, 128), BF16), pltpu.VMEM((win, 128), BF16), pltpu.VMEM((win, 128), BF16),
                        pltpu.VMEM((SEQ_BLOCK, 128), F32), pltpu.VMEM((win, 128), F32), pltpu.VMEM((win, 128), F32),
                        [pltpu.VMEM((SEQ_BLOCK, 128), F32)] * 3,
                        [pltpu.VMEM((SEQ_BLOCK, 128), F32)] * 3],
    )
    return pl.pallas_call(
        _attn_kernel,
        grid_spec=grid_spec,
        out_shape=jax.ShapeDtypeStruct((ntok, ATTN_WIDTH), BF16),
        compiler_params=pltpu.CompilerParams(dimension_semantics=("arbitrary", "arbitrary"),
                                             vmem_limit_bytes=VMEM_LIMIT),
        name="attn",
    )(has_prev, has_next, q, k, k, k, v, v, v, attn_beta.reshape(1, -1), _attn_bias_table())


def _block_diag2(m):
    z = jnp.zeros_like(m[0])
    return jnp.concatenate([jnp.concatenate([m[0], z], axis=1), jnp.concatenate([z, m[1]], axis=1)], axis=0)


def _head_ones():
    idx = np.arange(RWKV_WIDTH) // HEAD_DIM
    return jnp.asarray((idx[:, None] == idx[None, :]).astype(np.float32), dtype=BF16)


def _stack_heads(x):
    m0 = _lane_is_head0(x.shape)
    z = jnp.zeros_like(x)
    return jnp.concatenate([jnp.where(m0, x, z), jnp.where(m0, z, x)], axis=0)


def _stack_heads_bf16(x):
    return _stack_heads(x.astype(BF16))


def _scan_masks(rev):
    c = CHUNK
    rowi = lax.broadcasted_iota(jnp.int32, (c, 128), 0)
    colj = lax.broadcasted_iota(jnp.int32, (c, 128), 1) & (c - 1)
    ti = lax.broadcasted_iota(jnp.int32, (c, c), 0)
    tj = lax.broadcasted_iota(jnp.int32, (c, c), 1)
    if rev:
        strict, incl, tri = colj > rowi, colj >= rowi, tj >= ti
    else:
        strict, incl, tri = colj < rowi, colj <= rowi, tj <= ti
    return dict(strict=strict, incl=incl, tri=tri.astype(BF16), eye2=(colj == rowi).astype(F32))


def _scan_pre(ins, masks):
    c = CHUNK
    idx = range(len(ins))
    lw = [a[0] for a in ins]
    v = [a[3] for a in ins]

    cl = []
    for i in idx:
        hi, mid, _ = _split3(lw[i])
        cl2 = _dot(masks[i]["tri"], jnp.concatenate([hi, mid], axis=1))
        cl.append(cl2[:, 0:128] + cl2[:, 128:256])
    yield
    g_tot = [jnp.exp(cl[i][0:1, :] if masks[i]["rev"] else cl[i][c - 1:c, :]) for i in idx]
    g_inv = [jnp.exp(-cl[i]) for i in idx]
    rt = [ins[i][1] * jnp.exp(cl[i]) for i in idx]
    bt = [ins[i][2] * jnp.exp(cl[i] - lw[i]) for i in idx]
    at = [-(ins[i][5] * g_inv[i]) for i in idx]
    kt = [ins[i][4] * g_inv[i] for i in idx]
    br = [jnp.concatenate([bt[i], rt[i]], axis=0).astype(BF16) for i in idx]
    lm = [_dot_nt(br[i], jnp.concatenate([_stack_heads_bf16(at[i]), _stack_heads_bf16(kt[i])], axis=0))
          for i in idx]
    akg = [jnp.concatenate([at[i] * g_tot[i], kt[i] * g_tot[i]], axis=0).astype(BF16) for i in idx]
    yield
    la = [jnp.where(masks[i]["strict"], lm[i][0:c, 0:128], 0.0) for i in idx]
    lk = [jnp.where(masks[i]["strict"], lm[i][0:c, 128:256], 0.0).astype(BF16) for i in idx]
    mam = [jnp.concatenate([jnp.where(masks[i]["incl"], lm[i][c:2 * c, 0:128], 0.0),
                            jnp.where(masks[i]["incl"], lm[i][c:2 * c, 128:256], 0.0)], axis=1).astype(BF16)
           for i in idx]

    sv = [_stack_heads_bf16(v[i]) for i in idx]
    lkv = [_dot(lk[i], sv[i]) for i in idx]

    t = [masks[i]["eye2"] + la[i] for i in idx]
    pw = [_dot(la[i].astype(BF16), _stack_heads_bf16(la[i])) for i in idx]
    yield
    for step in range(5):
        if step < 4:
            out = [_dot(pw[i].astype(BF16),
                        jnp.concatenate([_stack_heads_bf16(pw[i]), _stack_heads_bf16(t[i])], axis=1))
                   for i in idx]
            t = [t[i] + out[i][:, 128:256] for i in idx]
            pw = [out[i][:, 0:128] for i in idx]
        else:
            t = [t[i] + _dot(pw[i].astype(BF16), _stack_heads_bf16(t[i])) for i in idx]
        yield
    return [dict(br=br[i], lkv=lkv[i], mam=mam[i], sv=sv[i], t=t[i].astype(BF16), v=v[i].astype(BF16),
                 akg=akg[i], g_tot=g_tot[i]) for i in idx]


def _scan_seq(pre, zs, n_chunks):
    c = CHUNK
    per = len(zs)
    idx = range(per)
    zr = lax.broadcasted_iota(jnp.int32, (128, 128), 0) < HEAD_DIM
    zc = lax.broadcasted_iota(jnp.int32, (128, 128), 1) < HEAD_DIM
    same_head = zr == zc
    ys = []
    for k in range(n_chunks):
        p = pre[k * per:(k + 1) * per]
        brz = [_dot_nt(p[i]["br"], zs[i].astype(BF16)) for i in idx]
        yield
        u = [_dot(p[i]["t"], _stack_heads_bf16(brz[i][0:c] + p[i]["lkv"])) for i in idx]
        yield
        ys += [brz[i][c:2 * c] + _dot(p[i]["mam"], jnp.concatenate([_stack_heads_bf16(u[i]), p[i]["sv"]], axis=0))
               for i in idx]
        zs = [zs[i] * p[i]["g_tot"]
              + jnp.where(same_head, _dot_tn(jnp.concatenate([u[i].astype(BF16), p[i]["v"]], axis=0), p[i]["akg"]), 0.0)
              for i in idx]
        yield
    return ys, zs


def _run(gen):
    try:
        while True:
            next(gen)
    except StopIteration as done:
        return done.value


def _interleave(gen_a, gen_b):
    live = {"a": gen_a, "b": gen_b}
    result = {}
    while live:
        for name in list(live):
            try:
                next(live[name])
            except StopIteration as done:
                result[name] = done.value
                del live[name]
    return result["a"], result["b"]


def _scan_kernel(bmap_ref, reset_ref,
                 rf, kkf, vf, lwf, kdf, bdf, rb, kkb, vb, lwb, kdb, bdb,
                 yf_ref, yb_ref, zf, zb):
    del bmap_ref
    g = pl.program_id(0)

    @pl.when(reset_ref[g] == 1)
    def _():
        zf[...] = jnp.zeros_like(zf)
        zb[...] = jnp.zeros_like(zb)

    nchunk = SCAN_BLOCK // CHUNK
    n_groups = nchunk // CHUNKS_PER_GROUP
    mask_f = dict(_scan_masks(False), rev=False)
    mask_b = dict(_scan_masks(True), rev=True)
    dirs = ((mask_f, (lwf, rf, kkf, vf, kdf, bdf), yf_ref, zf),
            (mask_b, (lwb, rb, kkb, vb, kdb, bdb), yb_ref, zb))

    def group_items(gi):
        ins, masks, outs = [], [], []
        for k in range(CHUNKS_PER_GROUP):
            step = gi * CHUNKS_PER_GROUP + k
            for mask, refs, y_ref, _ in dirs:
                cc = (nchunk - 1 - step) if mask["rev"] else step
                rows = slice(cc * CHUNK, (cc + 1) * CHUNK)
                for pr in range(N_PAIRS):
                    lanes = slice(128 * pr, 128 * (pr + 1))
                    ins.append([ref[rows, lanes] for ref in refs])
                    masks.append(mask)
                    outs.append((y_ref, rows, lanes))
        return ins, masks, outs

    zs = [z_ref[pr] for _, _, _, z_ref in dirs for pr in range(N_PAIRS)]
    ins, masks, outs = group_items(0)
    pre = _run(_scan_pre(ins, masks))
    for gi in range(n_groups):
        seq = _scan_seq(pre, zs, CHUNKS_PER_GROUP)
        if gi + 1 < n_groups:
            ins, masks, next_outs = group_items(gi + 1)
            pre, (ys, zs) = _interleave(_scan_pre(ins, masks), seq)
        else:
            ys, zs = _run(seq)
        for (y_ref, rows, lanes), y in zip(outs, ys):
            y_ref[rows, lanes] = y
        if gi + 1 < n_groups:
            outs = next_outs
    state_refs = [(z_ref, pr) for _, _, _, z_ref in dirs for pr in range(N_PAIRS)]
    for (z_ref, pr), z in zip(state_refs, zs):
        z_ref[pr] = z


def _scan(r, kk, v, lw, kd, bd, bmap_bwd, reset):
    ntok = r.shape[0]
    c = RWKV_WIDTH
    ns = ntok // SCAN_BLOCK
    fwd = lambda g, bm, rs: (g, 0)
    bwd = lambda g, bm, rs: (bm[g], 0)
    fwd2 = lambda g, bm, rs: (0, g, 0)
    bwd2 = lambda g, bm, rs: (1, bm[g], 0)
    tokf = pl.BlockSpec((SCAN_BLOCK, c), fwd)
    tokb = pl.BlockSpec((SCAN_BLOCK, c), bwd)
    dirf = pl.BlockSpec((None, SCAN_BLOCK, c), fwd2)
    dirb = pl.BlockSpec((None, SCAN_BLOCK, c), bwd2)
    grid_spec = pltpu.PrefetchScalarGridSpec(
        num_scalar_prefetch=2,
        grid=(ns,),
        in_specs=[tokf, tokf, tokf, dirf, dirf, dirf, tokb, tokb, tokb, dirb, dirb, dirb],
        out_specs=[tokf, tokb],
        scratch_shapes=[pltpu.VMEM((N_PAIRS, 128, 128), F32), pltpu.VMEM((N_PAIRS, 128, 128), F32)],
    )
    tok = jax.ShapeDtypeStruct((ntok, c), F32)
    return pl.pallas_call(
        _scan_kernel,
        grid_spec=grid_spec,
        out_shape=[tok, tok],
        compiler_params=pltpu.CompilerParams(dimension_semantics=("arbitrary",), vmem_limit_bytes=VMEM_LIMIT),
        name="rwkv_scan",
    )(bmap_bwd, reset, r, kk, v, lw, kd, bd, r, kk, v, lw, kd, bd)


def _post_ffn_kernel(n_prompt_tiles, mrow_ref, xp_ref, xs_ref, yf_ref, yb_ref, bonus_ref, g_ref, attn_ref,
                     gt1_ref, sh2_ref, sc2_ref, gt2_ref, g2_ref, lnw_ref, lnb_ref, ones_ref, wo_ref, w1_ref, w2_ref,
                     op_ref, os_ref):
    del mrow_ref
    is_prompt = pl.program_id(0) < n_prompt_tiles
    x = jnp.where(is_prompt, xp_ref[...], xs_ref[...])

    y = yf_ref[...] + yb_ref[...]
    ones_bd = ones_ref[...]
    mu = _dot(y.astype(BF16), ones_bd) * (1.0 / HEAD_DIM)
    d = y - mu
    var = _dot((d * d).astype(BF16), ones_bd) * (1.0 / HEAD_DIM)
    yn = d * lax.rsqrt(var + LN_X_EPS) * lnw_ref[...] + lnb_ref[...]
    rw = ((yn + bonus_ref[...]) * g_ref[...]).astype(BF16)

    a = ATTN_WIDTH
    mix = _dot(attn_ref[...], wo_ref[0:a, :]) + _dot(rw, wo_ref[a:, :])
    x1 = x + gt1_ref[0] * mix
    h = _modulated_rmsnorm(x1, g2_ref[...], sc2_ref[0], sh2_ref[0]).astype(BF16)

    acc = jnp.zeros((h.shape[0], D_MODEL), F32)
    for j in range(D_FF // FF_STEP):
        cols = slice(j * FF_STEP, (j + 1) * FF_STEP)
        act = jnp.maximum(_dot(h, w1_ref[:, cols]), 0.0)
        acc = acc + _dot((act * act).astype(BF16), w2_ref[cols, :])
    out = x1 + gt2_ref[0] * acc

    @pl.when(is_prompt)
    def _():
        op_ref[...] = out

    @pl.when(jnp.logical_not(is_prompt))
    def _():
        os_ref[...] = out


def _post_ffn(xp, xs, yf, yb, bonus, g, attn, mod3, mrow_tile, g_norm2, ln_x_w, ln_x_b, w_out_bf, w1_bf, w2_bf):
    n_prompt_tiles = xp.shape[0] // ROW_TILE
    nt = (xp.shape[0] + xs.shape[0]) // ROW_TILE
    c = RWKV_WIDTH
    row = lambda i, mr: (i, 0)
    const = lambda i, mr: (0, 0)
    resident = lambda shape: pl.BlockSpec(shape, const, pipeline_mode=pl.Buffered(1))
    modspec = lambda j: pl.BlockSpec((1, 1, D_MODEL), lambda i, mr: (mr[i] * 6 + j, 0, 0))
    xp_spec, xs_spec = _two_group_specs(n_prompt_tiles)
    half = pl.BlockSpec((ROW_TILE, c), row)
    grid_spec = pltpu.PrefetchScalarGridSpec(
        num_scalar_prefetch=1,
        grid=(nt,),
        in_specs=[xp_spec, xs_spec, half, half, half, half, half,
                  modspec(2), modspec(3), modspec(4), modspec(5),
                  pl.BlockSpec((1, D_MODEL), const), pl.BlockSpec((1, c), const), pl.BlockSpec((1, c), const),
                  resident((c, c)), resident((D_MODEL, D_MODEL)), resident((D_MODEL, D_FF)),
                  resident((D_FF, D_MODEL))],
        out_specs=[xp_spec, xs_spec],
    )
    return pl.pallas_call(
        functools.partial(_post_ffn_kernel, n_prompt_tiles),
        grid_spec=grid_spec,
        out_shape=[jax.ShapeDtypeStruct(xp.shape, F32), jax.ShapeDtypeStruct(xs.shape, F32)],
        compiler_params=pltpu.CompilerParams(dimension_semantics=("arbitrary",), vmem_limit_bytes=VMEM_LIMIT),
        name="post_ffn",
    )(mrow_tile, xp, xs, yf, yb, bonus, g, attn, mod3, mod3, mod3, mod3, g_norm2.reshape(1, -1),
      ln_x_w.reshape(1, -1), ln_x_b.reshape(1, -1), _head_ones(), w_out_bf, w1_bf, w2_bf)


def _sequence_tables(seq_blocks):
    mod_row, has_prev, has_next = [], [], []
    for s, n in enumerate(seq_blocks):
        for b in range(n):
            mod_row.append(s)
            has_prev.append(int(b > 0))
            has_next.append(int(b < n - 1))
    per = SEQ_BLOCK // SCAN_BLOCK
    bmap_bwd, reset = [], []
    start = 0
    for n in seq_blocks:
        ns = n * per
        for t in range(ns):
            bmap_bwd.append(start + ns - 1 - t)
            reset.append(int(t == 0))
        start += ns
    i32 = lambda a: jnp.asarray(np.asarray(a, np.int32))
    rep = lambda a, k: np.repeat(np.asarray(a, np.int32), k)
    row_tiles = SEQ_BLOCK // ROW_TILE
    first = np.zeros(len(mod_row) * row_tiles, np.int32)
    last = np.zeros(len(mod_row) * row_tiles, np.int32)
    for b in range(len(mod_row)):
        if not has_prev[b]:
            first[b * row_tiles] = 1
        if not has_next[b]:
            last[(b + 1) * row_tiles - 1] = 1
    return dict(mrow_tile=i32(rep(mod_row, row_tiles)), has_prev=i32(has_prev), has_next=i32(has_next),
                bmap_bwd=i32(bmap_bwd), reset=i32(reset), first=i32(first), last=i32(last))


def _layer(xp, xs, c_all, tabs, w_ada, b_ada, g_norm1, g_norm2, w_in, q_norm_g, k_norm_g, attn_beta,
           mu_prev, mu_next, w0, w_up, a0, a_up, g_up, k_k, k_a, r_k, ln_x_w, ln_x_b, w_out, w_ff1, w_ff2):
    mod = _ada(c_all, w_ada, b_ada)
    mod3 = mod.reshape(-1, 1, D_MODEL)
    q, k, v, r, kk, vv, g, bonus, lw, kd, bd = _inproj(
        xp, xs, mod3, tabs, g_norm1, w_in.astype(BF16), q_norm_g, k_norm_g,
        mu_prev, mu_next, w0, w_up, a0, a_up, g_up, k_k, k_a, r_k)
    attn = _attention(q, k, v, tabs["has_prev"], tabs["has_next"], attn_beta)
    yf, yb = _scan(r, kk, vv, lw, kd, bd, tabs["bmap_bwd"], tabs["reset"])
    return _post_ffn(xp, xs, yf, yb, bonus, g, attn, mod3, tabs["mrow_tile"], g_norm2, ln_x_w, ln_x_b,
                     w_out.astype(BF16), w_ff1.astype(BF16), w_ff2.astype(BF16))


def kernel(x_prompt, x_sample, c_prompt, c_sample, w_ada, b_ada, g_norm1, g_norm2, w_in, q_norm_g, k_norm_g, attn_beta, mu_prev, mu_next, w0, w_up, a0, a_up, g_up, k_k, k_a, r_k, ln_x_w, ln_x_b, w_out, w_ff1, w_ff2):
    bp, sp, d = x_prompt.shape
    bs, ss, _ = x_sample.shape
    assert d == D_MODEL and sp % SEQ_BLOCK == 0 and ss % SEQ_BLOCK == 0
    seq_blocks = [sp // SEQ_BLOCK] * bp + [ss // SEQ_BLOCK] * bs
    tabs = _sequence_tables(seq_blocks)
    n_seq = bp + bs
    pad = (-n_seq) % 8
    c_all = jnp.concatenate([c_prompt, c_sample, jnp.zeros((pad, d), F32)], axis=0)
    yp = x_prompt.reshape(bp * sp, d)
    ys = x_sample.reshape(bs * ss, d)
    for i in range(w_ada.shape[0]):
        yp, ys = _layer(yp, ys, c_all, tabs, w_ada[i], b_ada[i], g_norm1[i], g_norm2[i], w_in[i], q_norm_g[i],
                        k_norm_g[i], attn_beta[i], mu_prev[i], mu_next[i], w0[i], w_up[i], a0[i], a_up[i], g_up[i],
                        k_k[i], k_a[i], r_k[i], ln_x_w[i], ln_x_b[i], w_out[i], w_ff1[i], w_ff2[i])
    return (yp.reshape(bp, sp, d), ys.reshape(bs, ss, d))
```
